```python
import math
import jax, jax.numpy as jnp
from jax import lax
import numpy as np

D_MODEL = 1024
BATCH = 8
SEQ = 2048
DEPTH = 1
DEC_BATCH = 128
DEC_SEQ = 1
PAST_LEN = 16384
PAGE_SIZE = 128

MIX_WIDTH = D_MODEL
SSM_WIDTH = MIX_WIDTH // 2
POOL_WIDTH = MIX_WIDTH - SSM_WIDTH
SSM_GROUP_CH = 16
SSM_GROUPS = SSM_WIDTH // SSM_GROUP_CH
SSM_STATE = 64
POOL_WINDOWS = (2, 4, 8, 16)
POOL_GROUPS = len(POOL_WINDOWS)
POOL_GROUP_CH = POOL_WIDTH // POOL_GROUPS
POOL_BUF = max(POOL_WINDOWS) - 1
IN_WIDTH = 2 * SSM_WIDTH + 2 * POOL_WIDTH
DN_ALPHA = (2.0 * DEPTH) ** 0.25
DN_BETA = (8.0 * DEPTH) ** -0.25
LN_EPS = 1e-5
DT_MIN = 1e-3
DT_MAX = 1e-1

kernel_name = "hybrid_s5_pool_deepnorm_step"


def _layer_norm(x, g, b):
    x = x.astype(jnp.float32)
    mu = jnp.mean(x, axis=-1, keepdims=True)
    var = jnp.mean(jnp.square(x - mu), axis=-1, keepdims=True)
    return (x - mu) * lax.rsqrt(var + LN_EPS) * g.astype(jnp.float32) + b.astype(jnp.float32)


def _cmul(ar, ai, br, bi):
    return ar * br - ai * bi, ar * bi + ai * br


def _scan_combine(e1, e2):
    a1r, a1i, b1r, b1i = e1
    a2r, a2i, b2r, b2i = e2
    ar, ai = _cmul(a2r, a2i, a1r, a1i)
    br, bi = _cmul(a2r, a2i, b1r, b1i)
    return ar, ai, br + b2r, bi + b2i


def _s5_mixer(u, h0r, h0i, lam_re, lam_im, log_dt, b_re, b_im, c_re, c_im, d):
    bsz, L, _ = u.shape
    f32 = jnp.float32
    lam_re = lam_re.astype(f32); lam_im = lam_im.astype(f32)
    ug = u.reshape(bsz, L, SSM_GROUPS, SSM_GROUP_CH)
    dt = jnp.exp(log_dt.astype(f32))[:, None]
    mag = jnp.exp(lam_re * dt)
    ar, ai = mag * jnp.cos(lam_im * dt), mag * jnp.sin(lam_im * dt)
    den = lam_re * lam_re + lam_im * lam_im
    nr = ar - 1.0
    qr = (nr * lam_re + ai * lam_im) / den
    qi = (ai * lam_re - nr * lam_im) / den
    b_re = b_re.astype(f32); b_im = b_im.astype(f32)
    bbar_re = qr[..., None] * b_re - qi[..., None] * b_im
    bbar_im = qr[..., None] * b_im + qi[..., None] * b_re
    bu_re = jnp.einsum('blgc,gnc->blgn', ug, bbar_re)
    bu_im = jnp.einsum('blgc,gnc->blgn', ug, bbar_im)
    a_re = jnp.broadcast_to(ar, bu_re.shape)
    a_im = jnp.broadcast_to(ai, bu_im.shape)
    _, _, hr, hi = lax.associative_scan(_scan_combine, (a_re, a_im, bu_re, bu_im), axis=1)
    k = jnp.arange(1, L + 1, dtype=f32)[:, None, None]
    pmag = jnp.exp(lam_re * dt * k)
    ph = lam_im * dt * k
    pr, pi_ = pmag * jnp.cos(ph), pmag * jnp.sin(ph)
    h0r = h0r.astype(f32)[:, None]; h0i = h0i.astype(f32)[:, None]
    hr = hr + pr * h0r - pi_ * h0i
    hi = hi + pr * h0i + pi_ * h0r
    y = (jnp.einsum('blgn,gcn->blgc', hr, c_re.astype(f32))
         - jnp.einsum('blgn,gcn->blgc', hi, c_im.astype(f32)))
    y = y.reshape(bsz, L, SSM_WIDTH) + d.astype(f32) * u
    return y, hr[:, -1], hi[:, -1]


def _pool_mixer(u, prefix, start_pos, w_pool, scale):
    bsz, L, _ = u.shape
    z = jnp.concatenate([prefix.astype(jnp.float32), u], axis=1)
    cs = jnp.concatenate([jnp.zeros((bsz, 1, POOL_WIDTH), jnp.float32),
                          jnp.cumsum(z, axis=1)], axis=1)
    pos = start_pos + jnp.arange(L)
    outs = []
    for g, w in enumerate(POOL_WINDOWS):
        lo_c, hi_c = g * POOL_GROUP_CH, (g + 1) * POOL_GROUP_CH
        s_hi = cs[:, POOL_BUF + 1:POOL_BUF + 1 + L, lo_c:hi_c]
        s_lo = cs[:, POOL_BUF + 1 - w:POOL_BUF + 1 - w + L, lo_c:hi_c]
        cnt = jnp.minimum(pos + 1, w).astype(jnp.float32)[None, :, None]
        outs.append((s_hi - s_lo) / cnt - u[..., lo_c:hi_c])
    pooled = jnp.stack(outs, axis=2)
    mixed = jnp.einsum('blgc,gcd->blgd', pooled, w_pool.astype(jnp.float32))
    mixed = mixed.reshape(bsz, L, POOL_WIDTH) * scale.astype(jnp.float32)
    return mixed, z[:, -POOL_BUF:]


def _layer(x, h0r, h0i, pool_prefix, start_pos, w_in, lam_re, lam_im, log_dt, b_re, b_im,
           c_re, c_im, d, glu_w, glu_b, pool_w, pool_scale, w_out, ln_g, ln_b):
    h = x.astype(jnp.float32)
    proj = h @ w_in.astype(jnp.float32)
    s_in, s_gate, p_in, p_gate = jnp.split(
        proj, [SSM_WIDTH, 2 * SSM_WIDTH, 2 * SSM_WIDTH + POOL_WIDTH], axis=-1)
    sy, hr, hi = _s5_mixer(s_in, h0r, h0i, lam_re, lam_im, log_dt, b_re, b_im, c_re, c_im, d)
    sy = jax.nn.gelu(sy)
    sy = sy * jax.nn.sigmoid(sy @ glu_w.astype(jnp.float32) + glu_b.astype(jnp.float32))
    py, buf = _pool_mixer(p_in, pool_prefix, start_pos, pool_w, pool_scale)
    mixed = jnp.concatenate([sy * jax.nn.silu(s_gate), py * jax.nn.silu(p_gate)], axis=-1)
    out = mixed @ w_out.astype(jnp.float32)
    y = _layer_norm(DN_ALPHA * h + out, ln_g, ln_b)
    return y, hr, hi, buf


def setup_inputs(seed: int = 0) -> dict:
    key = jax.random.key(seed)
    ks = jax.random.split(key, 24)
    f32 = jnp.float32
    n = jnp.arange(SSM_STATE, dtype=f32)
    lam_re = -0.5 + 0.01 * jax.random.normal(ks[0], (DEPTH, SSM_GROUPS, SSM_STATE), f32)
    lam_im = math.pi * n + 0.01 * jax.random.normal(ks[1], (DEPTH, SSM_GROUPS, SSM_STATE), f32)
    log_dt = jax.random.uniform(ks[2], (DEPTH, SSM_GROUPS), f32,
                                math.log(DT_MIN), math.log(DT_MAX))
    return {
        "x_prompt": jax.random.normal(ks[3], (BATCH, SEQ, D_MODEL), f32),
        "x_sample": jax.random.normal(ks[4], (DEC_BATCH, DEC_SEQ, D_MODEL), f32),
        "state_ssm_re": 0.5 * jax.random.normal(ks[5], (DEPTH, DEC_BATCH, SSM_GROUPS, SSM_STATE), f32),
        "state_ssm_im": 0.5 * jax.random.normal(ks[6], (DEPTH, DEC_BATCH, SSM_GROUPS, SSM_STATE), f32),
        "state_pool": jax.random.normal(ks[7], (DEPTH, DEC_BATCH, POOL_BUF, POOL_WIDTH), f32),
        "w_in": jax.random.normal(ks[8], (DEPTH, D_MODEL, IN_WIDTH), f32) * D_MODEL ** -0.5,
        "ssm_lambda_re": lam_re,
        "ssm_lambda_im": lam_im,
        "ssm_log_dt": log_dt,
        "ssm_b_re": jax.random.normal(ks[9], (DEPTH, SSM_GROUPS, SSM_STATE, SSM_GROUP_CH), f32) * (2 * SSM_GROUP_CH) ** -0.5,
        "ssm_b_im": jax.random.normal(ks[10], (DEPTH, SSM_GROUPS, SSM_STATE, SSM_GROUP_CH), f32) * (2 * SSM_GROUP_CH) ** -0.5,
        "ssm_c_re": jax.random.normal(ks[11], (DEPTH, SSM_GROUPS, SSM_GROUP_CH, SSM_STATE), f32) * (2 * SSM_STATE) ** -0.5,
        "ssm_c_im": jax.random.normal(ks[12], (DEPTH, SSM_GROUPS, SSM_GROUP_CH, SSM_STATE), f32) * (2 * SSM_STATE) ** -0.5,
        "ssm_d": jax.random.normal(ks[13], (DEPTH, SSM_WIDTH), f32),
        "glu_w": jax.random.normal(ks[14], (DEPTH, SSM_WIDTH, SSM_WIDTH), f32) * SSM_WIDTH ** -0.5,
        "glu_b": 0.01 * jax.random.normal(ks[15], (DEPTH, SSM_WIDTH), f32),
        "pool_w": jax.random.normal(ks[16], (DEPTH, POOL_GROUPS, POOL_GROUP_CH, POOL_GROUP_CH), f32) * POOL_GROUP_CH ** -0.5,
        "pool_scale": 1.0 + 0.02 * jax.random.normal(ks[17], (DEPTH, POOL_WIDTH), f32),
        "w_out": jax.random.normal(ks[18], (DEPTH, MIX_WIDTH, D_MODEL), f32) * (MIX_WIDTH ** -0.5 * DN_BETA),
        "ln_g": 1.0 + 0.02 * jax.random.normal(ks[19], (DEPTH, D_MODEL), f32),
        "ln_b": 0.01 * jax.random.normal(ks[20], (DEPTH, D_MODEL), f32),
    }


def reference(x_prompt, x_sample, state_ssm_re, state_ssm_im, state_pool, w_in,
              ssm_lambda_re, ssm_lambda_im, ssm_log_dt, ssm_b_re, ssm_b_im, ssm_c_re, ssm_c_im,
              ssm_d, glu_w, glu_b, pool_w, pool_scale, w_out, ln_g, ln_b):
    out_dtype = x_prompt.dtype
    hp = x_prompt.astype(jnp.float32)
    hs = x_sample.astype(jnp.float32)
    zeros_h = jnp.zeros((x_prompt.shape[0], SSM_GROUPS, SSM_STATE), jnp.float32)
    zeros_buf = jnp.zeros((x_prompt.shape[0], POOL_BUF, POOL_WIDTH), jnp.float32)
    p_re, p_im, p_buf, s_re, s_im, s_buf = [], [], [], [], [], []
    for l in range(DEPTH):
        weights = (w_in[l], ssm_lambda_re[l], ssm_lambda_im[l], ssm_log_dt[l], ssm_b_re[l],
                   ssm_b_im[l], ssm_c_re[l], ssm_c_im[l], ssm_d[l], glu_w[l], glu_b[l],
                   pool_w[l], pool_scale[l], w_out[l], ln_g[l], ln_b[l])
        hp, hr, hi, buf = _layer(hp, zeros_h, zeros_h, zeros_buf, 0, *weights)
        p_re.append(hr); p_im.append(hi); p_buf.append(buf)
        hs, hr, hi, buf = _layer(hs, state_ssm_re[l], state_ssm_im[l], state_pool[l], PAST_LEN, *weights)
        s_re.append(hr); s_im.append(hi); s_buf.append(buf)
    y_prompt = hp.astype(out_dtype)
    y_sample = hs.astype(out_dtype)
    new_ssm_re_prompt = jnp.stack(p_re)
    new_ssm_im_prompt = jnp.stack(p_im)
    new_pool_prompt = jnp.stack(p_buf)
    new_ssm_re_sample = jnp.stack(s_re)
    new_ssm_im_sample = jnp.stack(s_im)
    new_pool_sample = jnp.stack(s_buf)
    return (y_prompt, y_sample, new_ssm_re_prompt, new_ssm_im_prompt, new_pool_prompt,
            new_ssm_re_sample, new_ssm_im_sample, new_pool_sample)
```

```python
import functools
import math

import jax
import jax.numpy as jnp
from jax import lax
from jax.experimental import pallas as pl
from jax.experimental.pallas import tpu as pltpu

D_MODEL = 1024
SSM_WIDTH = 512
POOL_WIDTH = 512
SSM_GROUPS = 32
SSM_GROUP_CH = 16
SSM_STATE = 64
POOL_WINDOWS = (2, 4, 8, 16)
POOL_GROUP_CH = 128
POOL_BUF = 15
DN_ALPHA = 2.0 ** 0.25
LN_EPS = 1e-5

SUBLANES = 8
HALF_GROUPS = SSM_GROUPS // 2
HALF_STATE = HALF_GROUPS * SSM_STATE
STATE_COLS = 4 * HALF_STATE
CHUNK_TOKENS = 32
HALO_TOKENS = 16
VMEM_LIMIT_BYTES = 48 * 1024 * 1024

F32 = jnp.float32
BF16 = jnp.bfloat16


def _dot(a, b):
    return jnp.dot(a.astype(BF16), b, preferred_element_type=F32)


def _ssm_input(xb, w_s_ref, b_lo_ref, b_hi_ref):
    s_in = jnp.dot(xb, w_s_ref[...], preferred_element_type=F32)
    sb = s_in.astype(BF16)
    bu_lo = jnp.dot(sb[:, :256], b_lo_ref[...], preferred_element_type=F32)
    bu_hi = jnp.dot(sb[:, 256:], b_hi_ref[...], preferred_element_type=F32)
    return s_in, bu_lo, bu_hi


def _mix_and_norm(x, s_in, h, gates, pooled, c_lo_ref, c_hi_ref, d_ref, glu_w_ref, glu_b_ref,
                  wp_lo_ref, wp_hi_ref, pscale_ref, w_out_ref, ln_g_ref, ln_b_ref):
    hb = h.astype(BF16)
    y_lo = jnp.dot(hb[:, :2 * HALF_STATE], c_lo_ref[...], preferred_element_type=F32)
    y_hi = jnp.dot(hb[:, 2 * HALF_STATE:], c_hi_ref[...], preferred_element_type=F32)
    sy = jnp.concatenate([y_lo, y_hi], axis=-1) + d_ref[...] * s_in
    sy = jax.nn.gelu(sy)
    sy = sy * jax.nn.sigmoid(_dot(sy, glu_w_ref[...]) + glu_b_ref[...])
    s_gate = gates[:, :SSM_WIDTH]
    p_gate = gates[:, SSM_WIDTH + POOL_WIDTH:]
    pb = pooled.astype(BF16)
    py = jnp.concatenate([jnp.dot(pb[:, :256], wp_lo_ref[...], preferred_element_type=F32),
                          jnp.dot(pb[:, 256:], wp_hi_ref[...], preferred_element_type=F32)], axis=-1)
    py = py * pscale_ref[...]
    mixed = jnp.concatenate([sy * jax.nn.silu(s_gate), py * jax.nn.silu(p_gate)], axis=-1)
    r = DN_ALPHA * x + _dot(mixed, w_out_ref[...])
    mu = jnp.mean(r, axis=-1, keepdims=True)
    rc = r - mu
    var = jnp.mean(rc * rc, axis=-1, keepdims=True)
    return rc * lax.rsqrt(var + LN_EPS) * ln_g_ref[...] + ln_b_ref[...]


def _in_copies(x_hbm, xbuf, sem, chunk, slot):
    return [pltpu.make_async_copy(x_hbm.at[b, pl.ds(chunk * CHUNK_TOKENS, CHUNK_TOKENS), :],
                                  xbuf.at[slot, :, b, :], sem.at[slot]) for b in range(SUBLANES)]


def _out_copies(ybuf, y_hbm, sem, chunk, slot):
    return [pltpu.make_async_copy(ybuf.at[slot, :, b, :],
                                  y_hbm.at[b, pl.ds(chunk * CHUNK_TOKENS, CHUNK_TOKENS), :],
                                  sem.at[slot]) for b in range(SUBLANES)]


def _prompt_kernel(x_hbm, w_s_ref, w_g_ref, b_lo_ref, b_hi_ref, a_re_ref, a_im_ref, c_lo_ref, c_hi_ref,
                   d_ref, glu_w_ref, glu_b_ref, wp_lo_ref, wp_hi_ref, pscale_ref, w_out_ref, ln_g_ref,
                   ln_b_ref, y_hbm, hfin_ref, ptail_ref, xbuf, ybuf, hbuf, pbuf, sem_in, sem_out):
    s = pl.program_id(0)
    n = pl.num_programs(0)
    slot = lax.rem(s, 2)
    rows = CHUNK_TOKENS * SUBLANES
    halo = HALO_TOKENS * SUBLANES

    @pl.when(s == 0)
    def _():
        for c in _in_copies(x_hbm, xbuf, sem_in, 0, 0):
            c.start()
        hfin_ref[...] = jnp.zeros_like(hfin_ref)
        pbuf[0:halo, :] = jnp.zeros((halo, POOL_WIDTH), F32)

    @pl.when(s + 1 < n)
    def _():
        for c in _in_copies(x_hbm, xbuf, sem_in, s + 1, 1 - slot):
            c.start()

    for c in _in_copies(x_hbm, xbuf, sem_in, s, slot):
        c.wait()

    @pl.when(s >= 2)
    def _():
        for c in _out_copies(ybuf, y_hbm, sem_out, s - 2, slot):
            c.wait()

    x = xbuf[slot].reshape(rows, D_MODEL)
    xb = x.astype(BF16)
    s_in, bu_lo, bu_hi = _ssm_input(xb, w_s_ref, b_lo_ref, b_hi_ref)
    hbuf[:, :2 * HALF_STATE] = bu_lo
    hbuf[:, 2 * HALF_STATE:] = bu_hi

    def step(t, carry):
        r = pl.multiple_of(t * SUBLANES, SUBLANES)
        out = []
        for k in range(2):
            hr, hi = carry[2 * k], carry[2 * k + 1]
            ar = a_re_ref[:, k * HALF_STATE:(k + 1) * HALF_STATE]
            ai = a_im_ref[:, k * HALF_STATE:(k + 1) * HALF_STATE]
            c_re = 2 * k * HALF_STATE
            c_im = c_re + HALF_STATE
            nhr = ar * hr - ai * hi + hbuf[pl.ds(r, SUBLANES), c_re:c_re + HALF_STATE]
            nhi = ar * hi + ai * hr + hbuf[pl.ds(r, SUBLANES), c_im:c_im + HALF_STATE]
            hbuf[pl.ds(r, SUBLANES), c_re:c_re + HALF_STATE] = nhr
            hbuf[pl.ds(r, SUBLANES), c_im:c_im + HALF_STATE] = nhi
            out += [nhr, nhi]
        return tuple(out)

    init = tuple(hfin_ref[:, j * HALF_STATE:(j + 1) * HALF_STATE] for j in range(4))
    fin = lax.fori_loop(0, CHUNK_TOKENS, step, init, unroll=2)
    for j in range(4):
        hfin_ref[:, j * HALF_STATE:(j + 1) * HALF_STATE] = fin[j]

    gates = jnp.dot(xb, w_g_ref[...], preferred_element_type=F32)
    p_in = gates[:, SSM_WIDTH:SSM_WIDTH + POOL_WIDTH]
    pbuf[halo:halo + rows, :] = p_in

    row = lax.broadcasted_iota(jnp.int32, (rows, POOL_GROUP_CH), 0)
    pos1 = s * CHUNK_TOKENS + lax.shift_right_logical(row, 3) + 1
    pooled = []
    for g, w in enumerate(POOL_WINDOWS):
        acc = pbuf[:, g * POOL_GROUP_CH:(g + 1) * POOL_GROUP_CH]
        shift = SUBLANES
        while shift < w * SUBLANES:
            acc = acc[shift:, :] + acc[:-shift, :]
            shift *= 2
        acc = acc[acc.shape[0] - rows:, :]
        cnt = jnp.minimum(pos1, w).astype(F32)
        pooled.append(acc / cnt - p_in[:, g * POOL_GROUP_CH:(g + 1) * POOL_GROUP_CH])
    pooled = jnp.concatenate(pooled, axis=-1)
    pbuf[0:halo, :] = pbuf[rows:rows + halo, :]
    ptail_ref[...] = pbuf[0:halo, :]

    y = _mix_and_norm(x, s_in, hbuf[...], gates, pooled, c_lo_ref, c_hi_ref, d_ref, glu_w_ref, glu_b_ref,
                      wp_lo_ref, wp_hi_ref, pscale_ref, w_out_ref, ln_g_ref, ln_b_ref)
    ybuf[slot] = y.reshape(CHUNK_TOKENS, SUBLANES, D_MODEL)

    for c in _out_copies(ybuf, y_hbm, sem_out, s, slot):
        c.start()

    @pl.when(s == n - 1)
    def _():
        for c in _out_copies(ybuf, y_hbm, sem_out, s - 1, 1 - slot):
            c.wait()
        for c in _out_copies(ybuf, y_hbm, sem_out, s, slot):
            c.wait()


def _sample_kernel(x_ref, h0_ref, prefix_ref, w_s_ref, w_g_ref, b_lo_ref, b_hi_ref, a_re_ref, a_im_ref,
                   c_lo_ref, c_hi_ref, d_ref, glu_w_ref, glu_b_ref, wp_lo_ref, wp_hi_ref, pscale_ref,
                   w_out_ref, ln_g_ref, ln_b_ref, y_ref, hnew_ref, pin_ref):
    x = x_ref[...]
    xb = x.astype(BF16)
    s_in, bu_lo, bu_hi = _ssm_input(xb, w_s_ref, b_lo_ref, b_hi_ref)
    bu = (bu_lo, bu_hi)
    for k in range(2):
        c_re = 2 * k * HALF_STATE
        c_im = c_re + HALF_STATE
        ar = a_re_ref[0:1, k * HALF_STATE:(k + 1) * HALF_STATE]
        ai = a_im_ref[0:1, k * HALF_STATE:(k + 1) * HALF_STATE]
        hr = h0_ref[:, c_re:c_re + HALF_STATE]
        hi = h0_ref[:, c_im:c_im + HALF_STATE]
        hnew_ref[:, c_re:c_re + HALF_STATE] = ar * hr - ai * hi + bu[k][:, :HALF_STATE]
        hnew_ref[:, c_im:c_im + HALF_STATE] = ar * hi + ai * hr + bu[k][:, HALF_STATE:]

    gates = jnp.dot(xb, w_g_ref[...], preferred_element_type=F32)
    p_in = gates[:, SSM_WIDTH:SSM_WIDTH + POOL_WIDTH]
    pin_ref[...] = p_in
    pooled = []
    for g, w in enumerate(POOL_WINDOWS):
        cols = slice(g * POOL_GROUP_CH, (g + 1) * POOL_GROUP_CH)
        acc = p_in[:, cols]
        for j in range(1, w):
            acc = acc + prefix_ref[POOL_BUF - j, :, cols]
        pooled.append(acc / float(w) - p_in[:, cols])
    pooled = jnp.concatenate(pooled, axis=-1)
    y_ref[...] = _mix_and_norm(x, s_in, hnew_ref[...], gates, pooled, c_lo_ref, c_hi_ref, d_ref, glu_w_ref,
                               glu_b_ref, wp_lo_ref, wp_hi_ref, pscale_ref, w_out_ref, ln_g_ref, ln_b_ref)


def _block_diag(blocks):
    n, r, c = blocks.shape
    return jnp.einsum('grc,gh->grhc', blocks, jnp.eye(n, dtype=blocks.dtype)).reshape(n * r, n * c)


def _prepare_weights(w_in, lam_re, lam_im, log_dt, b_re, b_im, c_re, c_im, d, glu_w, glu_b, pool_w,
                     pool_scale, w_out, ln_g, ln_b):
    lam_re = lam_re.astype(F32)
    lam_im = lam_im.astype(F32)
    dt = jnp.exp(log_dt.astype(F32))[:, None]
    mag = jnp.exp(lam_re * dt)
    ar, ai = mag * jnp.cos(lam_im * dt), mag * jnp.sin(lam_im * dt)
    den = lam_re * lam_re + lam_im * lam_im
    nr = ar - 1.0
    qr = (nr * lam_re + ai * lam_im) / den
    qi = (ai * lam_re - nr * lam_im) / den
    b_re = b_re.astype(F32)
    b_im = b_im.astype(F32)
    bbar_re = qr[..., None] * b_re - qi[..., None] * b_im
    bbar_im = qr[..., None] * b_im + qi[..., None] * b_re
    halves = (slice(0, HALF_GROUPS), slice(HALF_GROUPS, SSM_GROUPS))
    b_mats = [jnp.concatenate([_block_diag(jnp.swapaxes(bbar_re[h], 1, 2)),
                               _block_diag(jnp.swapaxes(bbar_im[h], 1, 2))], axis=1).astype(BF16)
              for h in halves]
    c_re = c_re.astype(F32)
    c_im = c_im.astype(F32)
    c_mats = [jnp.concatenate([_block_diag(jnp.swapaxes(c_re[h], 1, 2)),
                               -_block_diag(jnp.swapaxes(c_im[h], 1, 2))], axis=0).astype(BF16)
              for h in halves]
    a_re = jnp.broadcast_to(ar.reshape(1, SSM_GROUPS * SSM_STATE), (SUBLANES, SSM_GROUPS * SSM_STATE))
    a_im = jnp.broadcast_to(ai.reshape(1, SSM_GROUPS * SSM_STATE), (SUBLANES, SSM_GROUPS * SSM_STATE))
    pw = pool_w.astype(F32)
    wp = [_block_diag(pw[0:2]).astype(BF16), _block_diag(pw[2:4]).astype(BF16)]
    row = lambda v: v.astype(F32).reshape(1, -1)
    w_in = w_in.astype(BF16)
    return dict(w_s=w_in[:, :SSM_WIDTH], w_g=w_in[:, SSM_WIDTH:], b_lo=b_mats[0], b_hi=b_mats[1],
                a_re=a_re, a_im=a_im, c_lo=c_mats[0], c_hi=c_mats[1], d=row(d),
                glu_w=glu_w.astype(BF16), glu_b=row(glu_b), wp_lo=wp[0], wp_hi=wp[1],
                pscale=row(pool_scale), w_out=w_out.astype(BF16), ln_g=row(ln_g), ln_b=row(ln_b))


_WEIGHT_ORDER = ('w_s', 'w_g', 'b_lo', 'b_hi', 'a_re', 'a_im', 'c_lo', 'c_hi', 'd', 'glu_w', 'glu_b',
                 'wp_lo', 'wp_hi', 'pscale', 'w_out', 'ln_g', 'ln_b')


def _to_state_cols(h_re, h_im):
    b = h_re.shape[0]
    re = h_re.reshape(b, 2, HALF_STATE)
    im = h_im.reshape(b, 2, HALF_STATE)
    return jnp.stack([re[:, 0], im[:, 0], re[:, 1], im[:, 1]], axis=1).reshape(b, STATE_COLS)


def _from_state_cols(h):
    b = h.shape[0]
    h4 = h.reshape(b, 4, HALF_STATE)
    re = jnp.stack([h4[:, 0], h4[:, 2]], axis=1).reshape(b, SSM_GROUPS, SSM_STATE)
    im = jnp.stack([h4[:, 1], h4[:, 3]], axis=1).reshape(b, SSM_GROUPS, SSM_STATE)
    return re, im


def _resident(shape):
    return pl.BlockSpec(shape, lambda s: (0,) * len(shape), pipeline_mode=pl.Buffered(1))


def _prompt_layer(x, w):
    bsz, seq, _ = x.shape
    assert bsz == SUBLANES and seq % CHUNK_TOKENS == 0 and seq >= 2 * CHUNK_TOKENS
    rows = CHUNK_TOKENS * SUBLANES
    halo = HALO_TOKENS * SUBLANES
    weights = [w[k] for k in _WEIGHT_ORDER]
    y, hfin, ptail = pl.pallas_call(
        _prompt_kernel,
        grid=(seq // CHUNK_TOKENS,),
        in_specs=[pl.BlockSpec(memory_space=pl.ANY)] + [_resident(a.shape) for a in weights],
        out_specs=[pl.BlockSpec(memory_space=pl.ANY),
                   pl.BlockSpec((SUBLANES, STATE_COLS), lambda s: (0, 0)),
                   pl.BlockSpec((halo, POOL_WIDTH), lambda s: (0, 0))],
        out_shape=[jax.ShapeDtypeStruct(x.shape, F32),
                   jax.ShapeDtypeStruct((SUBLANES, STATE_COLS), F32),
                   jax.ShapeDtypeStruct((halo, POOL_WIDTH), F32)],
        scratch_shapes=[pltpu.VMEM((2, CHUNK_TOKENS, SUBLANES, D_MODEL), F32),
                        pltpu.VMEM((2, CHUNK_TOKENS, SUBLANES, D_MODEL), F32),
                        pltpu.VMEM((rows, STATE_COLS), F32),
                        pltpu.VMEM((halo + rows, POOL_WIDTH), F32),
                        pltpu.SemaphoreType.DMA((2,)),
                        pltpu.SemaphoreType.DMA((2,))],
        compiler_params=pltpu.CompilerParams(dimension_semantics=("arbitrary",),
                                             vmem_limit_bytes=VMEM_LIMIT_BYTES),
        name="prompt_layer",
    )(x, *weights)
    h_re, h_im = _from_state_cols(hfin)
    buf = ptail.reshape(HALO_TOKENS, SUBLANES, POOL_WIDTH)[HALO_TOKENS - POOL_BUF:]
    return y, h_re, h_im, jnp.swapaxes(buf, 0, 1)


def _sample_layer(x, h_re, h_im, pool_prefix, w):
    bsz = x.shape[0]
    weights = [w[k] for k in _WEIGHT_ORDER]
    h0 = _to_state_cols(h_re.astype(F32), h_im.astype(F32))
    prefix = jnp.swapaxes(pool_prefix.astype(F32), 0, 1)
    y, hnew, p_in = pl.pallas_call(
        _sample_kernel,
        out_shape=[jax.ShapeDtypeStruct((bsz, D_MODEL), F32),
                   jax.ShapeDtypeStruct((bsz, STATE_COLS), F32),
                   jax.ShapeDtypeStruct((bsz, POOL_WIDTH), F32)],
        compiler_params=pltpu.CompilerParams(vmem_limit_bytes=VMEM_LIMIT_BYTES),
        name="sample_layer",
    )(x, h0, prefix, *weights)
    n_re, n_im = _from_state_cols(hnew)
    buf = jnp.concatenate([pool_prefix.astype(F32)[:, 1:], p_in[:, None, :]], axis=1)
    return y, n_re, n_im, buf


def kernel(x_prompt, x_sample, state_ssm_re, state_ssm_im, state_pool, w_in, ssm_lambda_re, ssm_lambda_im, ssm_log_dt, ssm_b_re, ssm_b_im, ssm_c_re, ssm_c_im, ssm_d, glu_w, glu_b, pool_w, pool_scale, w_out, ln_g, ln_b):
    out_dtype = x_prompt.dtype
    depth = w_in.shape[0]
    hp = x_prompt.astype(F32)
    hs = x_sample.astype(F32).reshape(x_sample.shape[0], D_MODEL)
    p_re, p_im, p_buf, s_re, s_im, s_buf = [], [], [], [], [], []
    for l in range(depth):
        w = _prepare_weights(w_in[l], ssm_lambda_re[l], ssm_lambda_im[l], ssm_log_dt[l], ssm_b_re[l],
                             ssm_b_im[l], ssm_c_re[l], ssm_c_im[l], ssm_d[l], glu_w[l], glu_b[l],
                             pool_w[l], pool_scale[l], w_out[l], ln_g[l], ln_b[l])
        hp, hr, hi, buf = _prompt_layer(hp, w)
        p_re.append(hr); p_im.append(hi); p_buf.append(buf)
        hs, hr, hi, buf = _sample_layer(hs, state_ssm_re[l], state_ssm_im[l], state_pool[l], w)
        s_re.append(hr); s_im.append(hi); s_buf.append(buf)
    return (hp.astype(out_dtype), hs.reshape(x_sample.shape).astype(out_dtype),
            jnp.stack(p_re), jnp.stack(p_im), jnp.stack(p_buf),
            jnp.stack(s_re), jnp.stack(s_im), jnp.stack(s_buf))
```

```python
import functools
import math

import jax
import jax.numpy as jnp
from jax import lax
from jax.experimental import pallas as pl
from jax.experimental.pallas import tpu as pltpu

D_MODEL = 1024
SSM_WIDTH = 512
POOL_WIDTH = 512
SSM_GROUPS = 32
SSM_GROUP_CH = 16
SSM_STATE = 64
POOL_WINDOWS = (2, 4, 8, 16)
POOL_GROUP_CH = 128
POOL_BUF = 15
DN_ALPHA = 2.0 ** 0.25
LN_EPS = 1e-5

SUBLANES = 8
MXU_TILE = 256
HALF_GROUPS = MXU_TILE // SSM_GROUP_CH
HALF_STATE = HALF_GROUPS * SSM_STATE
STATE_COLS = 4 * HALF_STATE
CHUNK_TOKENS = 64
HALO_TOKENS = 16
VMEM_LIMIT_BYTES = 48 * 1024 * 1024

F32 = jnp.float32
BF16 = jnp.bfloat16


def _dot(a, b):
    return jnp.dot(a.astype(BF16), b, preferred_element_type=F32)


def _ssm_input(xb, w_s_ref, b_lo_ref, b_hi_ref):
    s_in = jnp.dot(xb, w_s_ref[...], preferred_element_type=F32)
    sb = s_in.astype(BF16)
    bu_lo = jnp.dot(sb[:, :MXU_TILE], b_lo_ref[...], preferred_element_type=F32)
    bu_hi = jnp.dot(sb[:, MXU_TILE:], b_hi_ref[...], preferred_element_type=F32)
    return s_in, bu_lo, bu_hi


def _mix_and_norm(x, s_in, hb, gates, pooled, c_lo_ref, c_hi_ref, d_ref, glu_w_ref, glu_b_ref,
                  wp_lo_ref, wp_hi_ref, pscale_ref, w_out_ref, ln_g_ref, ln_b_ref):
    y_lo = jnp.dot(hb[:, :2 * HALF_STATE], c_lo_ref[...], preferred_element_type=F32)
    y_hi = jnp.dot(hb[:, 2 * HALF_STATE:], c_hi_ref[...], preferred_element_type=F32)
    sy = jnp.concatenate([y_lo, y_hi], axis=-1) + d_ref[...] * s_in
    sy = jax.nn.gelu(sy)
    sy = sy * jax.nn.sigmoid(_dot(sy, glu_w_ref[...]) + glu_b_ref[...])
    s_gate = gates[:, :SSM_WIDTH]
    p_gate = gates[:, SSM_WIDTH + POOL_WIDTH:]
    pb = pooled.astype(BF16)
    py = jnp.concatenate([jnp.dot(pb[:, :MXU_TILE], wp_lo_ref[...], preferred_element_type=F32),
                          jnp.dot(pb[:, MXU_TILE:], wp_hi_ref[...], preferred_element_type=F32)], axis=-1)
    py = py * pscale_ref[...]
    mixed = jnp.concatenate([sy * jax.nn.silu(s_gate), py * jax.nn.silu(p_gate)], axis=-1)
    r = DN_ALPHA * x + _dot(mixed, w_out_ref[...])
    mu = jnp.mean(r, axis=-1, keepdims=True)
    rc = r - mu
    var = jnp.mean(rc * rc, axis=-1, keepdims=True)
    return rc * lax.rsqrt(var + LN_EPS) * ln_g_ref[...] + ln_b_ref[...]


def _in_copies(x_hbm, xbuf, sem, chunk, slot):
    return [pltpu.make_async_copy(x_hbm.at[b, pl.ds(chunk * CHUNK_TOKENS, CHUNK_TOKENS), :],
                                  xbuf.at[slot, :, b, :], sem.at[slot]) for b in range(SUBLANES)]


def _out_copies(ybuf, y_hbm, sem, chunk, slot):
    return [pltpu.make_async_copy(ybuf.at[slot, :, b, :],
                                  y_hbm.at[b, pl.ds(chunk * CHUNK_TOKENS, CHUNK_TOKENS), :],
                                  sem.at[slot]) for b in range(SUBLANES)]


def _prompt_kernel(x_hbm, w_s_ref, w_g_ref, b_lo_ref, b_hi_ref, a_re_ref, a_im_ref, c_lo_ref, c_hi_ref,
                   d_ref, glu_w_ref, glu_b_ref, wp_lo_ref, wp_hi_ref, pscale_ref, w_out_ref, ln_g_ref,
                   ln_b_ref, y_hbm, hfin_ref, ptail_ref, xbuf, ybuf, hbuf, hb16, pbuf, sem_in, sem_out):
    s = pl.program_id(0)
    n = pl.num_programs(0)
    slot = lax.rem(s, 2)
    rows = CHUNK_TOKENS * SUBLANES
    halo = HALO_TOKENS * SUBLANES

    @pl.when(s == 0)
    def _():
        for c in _in_copies(x_hbm, xbuf, sem_in, 0, 0):
            c.start()
        hfin_ref[...] = jnp.zeros_like(hfin_ref)
        pbuf[0:halo, :] = jnp.zeros((halo, POOL_WIDTH), F32)

    @pl.when(s + 1 < n)
    def _():
        for c in _in_copies(x_hbm, xbuf, sem_in, s + 1, 1 - slot):
            c.start()

    for c in _in_copies(x_hbm, xbuf, sem_in, s, slot):
        c.wait()

    @pl.when(s >= 2)
    def _():
        for c in _out_copies(ybuf, y_hbm, sem_out, s - 2, slot):
            c.wait()

    x = xbuf[slot].reshape(rows, D_MODEL)
    xb = x.astype(BF16)
    s_in, bu_lo, bu_hi = _ssm_input(xb, w_s_ref, b_lo_ref, b_hi_ref)
    hbuf[:, :2 * HALF_STATE] = bu_lo
    hbuf[:, 2 * HALF_STATE:] = bu_hi

    gates = jnp.dot(xb, w_g_ref[...], preferred_element_type=F32)
    p_in = gates[:, SSM_WIDTH:SSM_WIDTH + POOL_WIDTH]
    pbuf[halo:halo + rows, :] = p_in

    state = [hfin_ref[:, j * HALF_STATE:(j + 1) * HALF_STATE] for j in range(4)]
    for t2 in range(CHUNK_TOKENS // 2):
        pair = []
        for t in (2 * t2, 2 * t2 + 1):
            r = t * SUBLANES
            for k in range(2):
                hr, hi = state[2 * k], state[2 * k + 1]
                ar = a_re_ref[:, k * HALF_STATE:(k + 1) * HALF_STATE]
                ai = a_im_ref[:, k * HALF_STATE:(k + 1) * HALF_STATE]
                c_re = 2 * k * HALF_STATE
                c_im = c_re + HALF_STATE
                state[2 * k] = ar * hr - ai * hi + hbuf[r:r + SUBLANES, c_re:c_re + HALF_STATE]
                state[2 * k + 1] = ar * hi + ai * hr + hbuf[r:r + SUBLANES, c_im:c_im + HALF_STATE]
            pair.append(jnp.concatenate(state, axis=-1))
        r2 = 2 * t2 * SUBLANES
        hb16[r2:r2 + 2 * SUBLANES, :] = jnp.concatenate(pair, axis=0).astype(BF16)
    for j in range(4):
        hfin_ref[:, j * HALF_STATE:(j + 1) * HALF_STATE] = state[j]

    row = lax.broadcasted_iota(jnp.int32, (rows, POOL_GROUP_CH), 0)
    pos1 = s * CHUNK_TOKENS + lax.shift_right_logical(row, 3) + 1
    pooled = []
    for g, w in enumerate(POOL_WINDOWS):
        acc = pbuf[:, g * POOL_GROUP_CH:(g + 1) * POOL_GROUP_CH]
        shift = SUBLANES
        while shift < w * SUBLANES:
            acc = acc[shift:, :] + acc[:-shift, :]
            shift *= 2
        acc = acc[acc.shape[0] - rows:, :]
        cnt = jnp.minimum(pos1, w).astype(F32)
        pooled.append(acc / cnt - p_in[:, g * POOL_GROUP_CH:(g + 1) * POOL_GROUP_CH])
    pooled = jnp.concatenate(pooled, axis=-1)
    pbuf[0:halo, :] = pbuf[rows:rows + halo, :]
    ptail_ref[...] = pbuf[0:halo, :]

    y = _mix_and_norm(x, s_in, hb16[...], gates, pooled, c_lo_ref, c_hi_ref, d_ref, glu_w_ref, glu_b_ref,
                      wp_lo_ref, wp_hi_ref, pscale_ref, w_out_ref, ln_g_ref, ln_b_ref)
    ybuf[slot] = y.reshape(CHUNK_TOKENS, SUBLANES, D_MODEL)

    for c in _out_copies(ybuf, y_hbm, sem_out, s, slot):
        c.start()

    @pl.when(s == n - 1)
    def _():
        for c in _out_copies(ybuf, y_hbm, sem_out, s - 1, 1 - slot):
            c.wait()
        for c in _out_copies(ybuf, y_hbm, sem_out, s, slot):
            c.wait()


def _sample_kernel(x_ref, h0_ref, prefix_ref, w_s_ref, w_g_ref, b_lo_ref, b_hi_ref, a_re_ref, a_im_ref,
                   c_lo_ref, c_hi_ref, d_ref, glu_w_ref, glu_b_ref, wp_lo_ref, wp_hi_ref, pscale_ref,
                   w_out_ref, ln_g_ref, ln_b_ref, y_ref, hnew_ref, pin_ref):
    x = x_ref[...]
    xb = x.astype(BF16)
    s_in, bu_lo, bu_hi = _ssm_input(xb, w_s_ref, b_lo_ref, b_hi_ref)
    bu = (bu_lo, bu_hi)
    for k in range(2):
        c_re = 2 * k * HALF_STATE
        c_im = c_re + HALF_STATE
        ar = a_re_ref[0:1, k * HALF_STATE:(k + 1) * HALF_STATE]
        ai = a_im_ref[0:1, k * HALF_STATE:(k + 1) * HALF_STATE]
        hr = h0_ref[:, c_re:c_re + HALF_STATE]
        hi = h0_ref[:, c_im:c_im + HALF_STATE]
        hnew_ref[:, c_re:c_re + HALF_STATE] = ar * hr - ai * hi + bu[k][:, :HALF_STATE]
        hnew_ref[:, c_im:c_im + HALF_STATE] = ar * hi + ai * hr + bu[k][:, HALF_STATE:]

    gates = jnp.dot(xb, w_g_ref[...], preferred_element_type=F32)
    p_in = gates[:, SSM_WIDTH:SSM_WIDTH + POOL_WIDTH]
    pin_ref[...] = p_in
    pooled = []
    for g, w in enumerate(POOL_WINDOWS):
        cols = slice(g * POOL_GROUP_CH, (g + 1) * POOL_GROUP_CH)
        acc = p_in[:, cols]
        for j in range(1, w):
            acc = acc + prefix_ref[POOL_BUF - j, :, cols]
        pooled.append(acc / float(w) - p_in[:, cols])
    pooled = jnp.concatenate(pooled, axis=-1)
    y_ref[...] = _mix_and_norm(x, s_in, hnew_ref[...].astype(BF16), gates, pooled, c_lo_ref, c_hi_ref, d_ref, glu_w_ref,
                               glu_b_ref, wp_lo_ref, wp_hi_ref, pscale_ref, w_out_ref, ln_g_ref, ln_b_ref)


def _block_diag(blocks):
    n, r, c = blocks.shape
    return jnp.einsum('grc,gh->grhc', blocks, jnp.eye(n, dtype=blocks.dtype)).reshape(n * r, n * c)


def _prepare_weights(w_in, lam_re, lam_im, log_dt, b_re, b_im, c_re, c_im, d, glu_w, glu_b, pool_w,
                     pool_scale, w_out, ln_g, ln_b):
    lam_re = lam_re.astype(F32)
    lam_im = lam_im.astype(F32)
    dt = jnp.exp(log_dt.astype(F32))[:, None]
    mag = jnp.exp(lam_re * dt)
    ar, ai = mag * jnp.cos(lam_im * dt), mag * jnp.sin(lam_im * dt)
    den = lam_re * lam_re + lam_im * lam_im
    nr = ar - 1.0
    qr = (nr * lam_re + ai * lam_im) / den
    qi = (ai * lam_re - nr * lam_im) / den
    b_re = b_re.astype(F32)
    b_im = b_im.astype(F32)
    bbar_re = qr[..., None] * b_re - qi[..., None] * b_im
    bbar_im = qr[..., None] * b_im + qi[..., None] * b_re
    halves = (slice(0, HALF_GROUPS), slice(HALF_GROUPS, SSM_GROUPS))
    b_mats = [jnp.concatenate([_block_diag(jnp.swapaxes(bbar_re[h], 1, 2)),
                               _block_diag(jnp.swapaxes(bbar_im[h], 1, 2))], axis=1).astype(BF16)
              for h in halves]
    c_re = c_re.astype(F32)
    c_im = c_im.astype(F32)
    c_mats = [jnp.concatenate([_block_diag(jnp.swapaxes(c_re[h], 1, 2)),
                               -_block_diag(jnp.swapaxes(c_im[h], 1, 2))], axis=0).astype(BF16)
              for h in halves]
    a_re = jnp.broadcast_to(ar.reshape(1, SSM_GROUPS * SSM_STATE), (SUBLANES, SSM_GROUPS * SSM_STATE))
    a_im = jnp.broadcast_to(ai.reshape(1, SSM_GROUPS * SSM_STATE), (SUBLANES, SSM_GROUPS * SSM_STATE))
    pw = pool_w.astype(F32)
    wp = [_block_diag(pw[0:2]).astype(BF16), _block_diag(pw[2:4]).astype(BF16)]
    row = lambda v: v.astype(F32).reshape(1, -1)
    w_in = w_in.astype(BF16)
    return dict(w_s=w_in[:, :SSM_WIDTH], w_g=w_in[:, SSM_WIDTH:], b_lo=b_mats[0], b_hi=b_mats[1],
                a_re=a_re, a_im=a_im, c_lo=c_mats[0], c_hi=c_mats[1], d=row(d),
                glu_w=glu_w.astype(BF16), glu_b=row(glu_b), wp_lo=wp[0], wp_hi=wp[1],
                pscale=row(pool_scale), w_out=w_out.astype(BF16), ln_g=row(ln_g), ln_b=row(ln_b))


_WEIGHT_ORDER = ('w_s', 'w_g', 'b_lo', 'b_hi', 'a_re', 'a_im', 'c_lo', 'c_hi', 'd', 'glu_w', 'glu_b',
                 'wp_lo', 'wp_hi', 'pscale', 'w_out', 'ln_g', 'ln_b')


def _to_state_cols(h_re, h_im):
    b = h_re.shape[0]
    re = h_re.reshape(b, 2, HALF_STATE)
    im = h_im.reshape(b, 2, HALF_STATE)
    return jnp.stack([re[:, 0], im[:, 0], re[:, 1], im[:, 1]], axis=1).reshape(b, STATE_COLS)


def _from_state_cols(h):
    b = h.shape[0]
    h4 = h.reshape(b, 4, HALF_STATE)
    re = jnp.stack([h4[:, 0], h4[:, 2]], axis=1).reshape(b, SSM_GROUPS, SSM_STATE)
    im = jnp.stack([h4[:, 1], h4[:, 3]], axis=1).reshape(b, SSM_GROUPS, SSM_STATE)
    return re, im


def _resident(shape):
    return pl.BlockSpec(shape, lambda s: (0,) * len(shape), pipeline_mode=pl.Buffered(1))


def _prompt_layer(x, w):
    bsz, seq, _ = x.shape
    assert bsz == SUBLANES and seq % CHUNK_TOKENS == 0 and seq >= 2 * CHUNK_TOKENS
    rows = CHUNK_TOKENS * SUBLANES
    halo = HALO_TOKENS * SUBLANES
    weights = [w[k] for k in _WEIGHT_ORDER]
    y, hfin, ptail = pl.pallas_call(
        _prompt_kernel,
        grid=(seq // CHUNK_TOKENS,),
        in_specs=[pl.BlockSpec(memory_space=pl.ANY)] + [_resident(a.shape) for a in weights],
        out_specs=[pl.BlockSpec(memory_space=pl.ANY),
                   pl.BlockSpec((SUBLANES, STATE_COLS), lambda s: (0, 0)),
                   pl.BlockSpec((halo, POOL_WIDTH), lambda s: (0, 0))],
        out_shape=[jax.ShapeDtypeStruct(x.shape, F32),
                   jax.ShapeDtypeStruct((SUBLANES, STATE_COLS), F32),
                   jax.ShapeDtypeStruct((halo, POOL_WIDTH), F32)],
        scratch_shapes=[pltpu.VMEM((2, CHUNK_TOKENS, SUBLANES, D_MODEL), F32),
                        pltpu.VMEM((2, CHUNK_TOKENS, SUBLANES, D_MODEL), F32),
                        pltpu.VMEM((rows, STATE_COLS), F32),
                        pltpu.VMEM((rows, STATE_COLS), BF16),
                        pltpu.VMEM((halo + rows, POOL_WIDTH), F32),
                        pltpu.SemaphoreType.DMA((2,)),
                        pltpu.SemaphoreType.DMA((2,))],
        compiler_params=pltpu.CompilerParams(dimension_semantics=("arbitrary",),
                                             vmem_limit_bytes=VMEM_LIMIT_BYTES),
        name="prompt_layer",
    )(x, *weights)
    h_re, h_im = _from_state_cols(hfin)
    buf = ptail.reshape(HALO_TOKENS, SUBLANES, POOL_WIDTH)[HALO_TOKENS - POOL_BUF:]
    return y, h_re, h_im, jnp.swapaxes(buf, 0, 1)


def _sample_layer(x, h_re, h_im, pool_prefix, w):
    bsz = x.shape[0]
    weights = [w[k] for k in _WEIGHT_ORDER]
    h0 = _to_state_cols(h_re.astype(F32), h_im.astype(F32))
    prefix = jnp.swapaxes(pool_prefix.astype(F32), 0, 1)
    y, hnew, p_in = pl.pallas_call(
        _sample_kernel,
        out_shape=[jax.ShapeDtypeStruct((bsz, D_MODEL), F32),
                   jax.ShapeDtypeStruct((bsz, STATE_COLS), F32),
                   jax.ShapeDtypeStruct((bsz, POOL_WIDTH), F32)],
        compiler_params=pltpu.CompilerParams(vmem_limit_bytes=VMEM_LIMIT_BYTES),
        name="sample_layer",
    )(x, h0, prefix, *weights)
    n_re, n_im = _from_state_cols(hnew)
    buf = jnp.concatenate([pool_prefix.astype(F32)[:, 1:], p_in[:, None, :]], axis=1)
    return y, n_re, n_im, buf


def kernel(x_prompt, x_sample, state_ssm_re, state_ssm_im, state_pool, w_in, ssm_lambda_re, ssm_lambda_im, ssm_log_dt, ssm_b_re, ssm_b_im, ssm_c_re, ssm_c_im, ssm_d, glu_w, glu_b, pool_w, pool_scale, w_out, ln_g, ln_b):
    out_dtype = x_prompt.dtype
    depth = w_in.shape[0]
    hp = x_prompt.astype(F32)
    hs = x_sample.astype(F32).reshape(x_sample.shape[0], D_MODEL)
    p_re, p_im, p_buf, s_re, s_im, s_buf = [], [], [], [], [], []
    for l in range(depth):
        w = _prepare_weights(w_in[l], ssm_lambda_re[l], ssm_lambda_im[l], ssm_log_dt[l], ssm_b_re[l],
                             ssm_b_im[l], ssm_c_re[l], ssm_c_im[l], ssm_d[l], glu_w[l], glu_b[l],
                             pool_w[l], pool_scale[l], w_out[l], ln_g[l], ln_b[l])
        hp, hr, hi, buf = _prompt_layer(hp, w)
        p_re.append(hr); p_im.append(hi); p_buf.append(buf)
        hs, hr, hi, buf = _sample_layer(hs, state_ssm_re[l], state_ssm_im[l], state_pool[l], w)
        s_re.append(hr); s_im.append(hi); s_buf.append(buf)
    return (hp.astype(out_dtype), hs.reshape(x_sample.shape).astype(out_dtype),
            jnp.stack(p_re), jnp.stack(p_im), jnp.stack(p_buf),
            jnp.stack(s_re), jnp.stack(s_im), jnp.stack(s_buf))
```

```python
import functools
import math

import jax
import jax.numpy as jnp
from jax import lax
from jax.experimental import pallas as pl
from jax.experimental.pallas import tpu as pltpu

D_MODEL = 1024
SSM_WIDTH = 512
POOL_WIDTH = 512
SSM_GROUPS = 32
SSM_GROUP_CH = 16
SSM_STATE = 64
POOL_WINDOWS = (2, 4, 8, 16)
POOL_GROUP_CH = 128
POOL_BUF = 15
DN_ALPHA = 2.0 ** 0.25
LN_EPS = 1e-5

SUBLANES = 8
MXU_TILE = 256
HALF_GROUPS = MXU_TILE // SSM_GROUP_CH
HALF_STATE = HALF_GROUPS * SSM_STATE
STATE_COLS = 4 * HALF_STATE
CHUNK_TOKENS = 64
HALO_TOKENS = 16
VMEM_LIMIT_BYTES = 48 * 1024 * 1024

F32 = jnp.float32
BF16 = jnp.bfloat16


def _dot(a, b):
    return jnp.dot(a.astype(BF16), b, preferred_element_type=F32)


def _ssm_input(xb, w_s_ref, b_lo_ref, b_hi_ref):
    s_in = jnp.dot(xb, w_s_ref[...], preferred_element_type=F32)
    sb = s_in.astype(BF16)
    bu_lo = jnp.dot(sb[:, :MXU_TILE], b_lo_ref[...], preferred_element_type=F32)
    bu_hi = jnp.dot(sb[:, MXU_TILE:], b_hi_ref[...], preferred_element_type=F32)
    return s_in, bu_lo, bu_hi


def _mix_and_norm(x, s_in, hb, gates, pooled, c_lo_ref, c_hi_ref, d_ref, glu_w_ref, glu_b_ref,
                  wp_lo_ref, wp_hi_ref, pscale_ref, w_out_ref, ln_g_ref, ln_b_ref):
    y_lo = jnp.dot(hb[:, :2 * HALF_STATE], c_lo_ref[...], preferred_element_type=F32)
    y_hi = jnp.dot(hb[:, 2 * HALF_STATE:], c_hi_ref[...], preferred_element_type=F32)
    sy = jnp.concatenate([y_lo, y_hi], axis=-1) + d_ref[...] * s_in
    sy = jax.nn.gelu(sy)
    sy = sy * jax.nn.sigmoid(_dot(sy, glu_w_ref[...]) + glu_b_ref[...])
    s_gate = gates[:, :SSM_WIDTH]
    p_gate = gates[:, SSM_WIDTH + POOL_WIDTH:]
    pb = pooled.astype(BF16)
    py = jnp.concatenate([jnp.dot(pb[:, :MXU_TILE], wp_lo_ref[...], preferred_element_type=F32),
                          jnp.dot(pb[:, MXU_TILE:], wp_hi_ref[...], preferred_element_type=F32)], axis=-1)
    py = py * pscale_ref[...]
    mixed = jnp.concatenate([sy * jax.nn.silu(s_gate), py * jax.nn.silu(p_gate)], axis=-1)
    r = DN_ALPHA * x + _dot(mixed, w_out_ref[...])
    mu = jnp.mean(r, axis=-1, keepdims=True)
    rc = r - mu
    var = jnp.mean(rc * rc, axis=-1, keepdims=True)
    return rc * lax.rsqrt(var + LN_EPS) * ln_g_ref[...] + ln_b_ref[...]


def _in_copies(x_hbm, xbuf, sem, chunk, slot):
    return [pltpu.make_async_copy(x_hbm.at[b, pl.ds(chunk * CHUNK_TOKENS, CHUNK_TOKENS), :],
                                  xbuf.at[slot, :, b, :], sem.at[slot]) for b in range(SUBLANES)]


def _out_copies(ybuf, y_hbm, sem, chunk, slot):
    return [pltpu.make_async_copy(ybuf.at[slot, :, b, :],
                                  y_hbm.at[b, pl.ds(chunk * CHUNK_TOKENS, CHUNK_TOKENS), :],
                                  sem.at[slot]) for b in range(SUBLANES)]


def _prompt_kernel(x_hbm, w_s_ref, w_g_ref, b_lo_ref, b_hi_ref, a_re_ref, a_im_ref, c_lo_ref, c_hi_ref,
                   d_ref, glu_w_ref, glu_b_ref, wp_lo_ref, wp_hi_ref, pscale_ref, w_out_ref, ln_g_ref,
                   ln_b_ref, y_hbm, hfin_ref, ptail_ref, xbuf, ybuf, hbuf, hb16, pbuf, sem_in, sem_out):
    s = pl.program_id(0)
    n = pl.num_programs(0)
    slot = lax.rem(s, 2)
    rows = CHUNK_TOKENS * SUBLANES
    halo = HALO_TOKENS * SUBLANES

    @pl.when(s == 0)
    def _():
        for c in _in_copies(x_hbm, xbuf, sem_in, 0, 0):
            c.start()
        hfin_ref[...] = jnp.zeros_like(hfin_ref)
        pbuf[0:halo, :] = jnp.zeros((halo, POOL_WIDTH), F32)

    @pl.when(s + 1 < n)
    def _():
        for c in _in_copies(x_hbm, xbuf, sem_in, s + 1, 1 - slot):
            c.start()

    for c in _in_copies(x_hbm, xbuf, sem_in, s, slot):
        c.wait()

    @pl.when(s >= 2)
    def _():
        for c in _out_copies(ybuf, y_hbm, sem_out, s - 2, slot):
            c.wait()

    x = xbuf[slot].reshape(rows, D_MODEL)
    xb = x.astype(BF16)
    s_in, bu_lo, bu_hi = _ssm_input(xb, w_s_ref, b_lo_ref, b_hi_ref)
    hbuf[:, :2 * HALF_STATE] = bu_lo
    hbuf[:, 2 * HALF_STATE:] = bu_hi

    gates = jnp.dot(xb, w_g_ref[...], preferred_element_type=F32)
    p_in = gates[:, SSM_WIDTH:SSM_WIDTH + POOL_WIDTH]
    pbuf[halo:halo + rows, :] = p_in

    state = [hfin_ref[:, j * HALF_STATE:(j + 1) * HALF_STATE] for j in range(4)]
    for t2 in range(CHUNK_TOKENS // 2):
        pair = []
        for t in (2 * t2, 2 * t2 + 1):
            r = t * SUBLANES
            for k in range(2):
                hr, hi = state[2 * k], state[2 * k + 1]
                ar = a_re_ref[:, k * HALF_STATE:(k + 1) * HALF_STATE]
                ai = a_im_ref[:, k * HALF_STATE:(k + 1) * HALF_STATE]
                c_re = 2 * k * HALF_STATE
                c_im = c_re + HALF_STATE
                state[2 * k] = ar * hr - ai * hi + hbuf[r:r + SUBLANES, c_re:c_re + HALF_STATE]
                state[2 * k + 1] = ar * hi + ai * hr + hbuf[r:r + SUBLANES, c_im:c_im + HALF_STATE]
            pair.append(jnp.concatenate(state, axis=-1))
        r2 = 2 * t2 * SUBLANES
        hb16[r2:r2 + 2 * SUBLANES, :] = jnp.concatenate(pair, axis=0).astype(BF16)
    for j in range(4):
        hfin_ref[:, j * HALF_STATE:(j + 1) * HALF_STATE] = state[j]

    row = lax.broadcasted_iota(jnp.int32, (rows, POOL_GROUP_CH), 0)
    pos1 = s * CHUNK_TOKENS + lax.shift_right_logical(row, 3) + 1
    pooled = []
    for g, w in enumerate(POOL_WINDOWS):
        acc = pbuf[:, g * POOL_GROUP_CH:(g + 1) * POOL_GROUP_CH]
        shift = SUBLANES
        while shift < w * SUBLANES:
            acc = acc[shift:, :] + acc[:-shift, :]
            shift *= 2
        acc = acc[acc.shape[0] - rows:, :]
        cnt = jnp.minimum(pos1, w).astype(F32)
        pooled.append(acc / cnt - p_in[:, g * POOL_GROUP_CH:(g + 1) * POOL_GROUP_CH])
    pooled = jnp.concatenate(pooled, axis=-1)
    pbuf[0:halo, :] = pbuf[rows:rows + halo, :]
    ptail_ref[...] = pbuf[0:halo, :]

    y = _mix_and_norm(x, s_in, hb16[...], gates, pooled, c_lo_ref, c_hi_ref, d_ref, glu_w_ref, glu_b_ref,
                      wp_lo_ref, wp_hi_ref, pscale_ref, w_out_ref, ln_g_ref, ln_b_ref)
    ybuf[slot] = y.reshape(CHUNK_TOKENS, SUBLANES, D_MODEL)

    for c in _out_copies(ybuf, y_hbm, sem_out, s, slot):
        c.start()

    @pl.when(s == n - 1)
    def _():
        for c in _out_copies(ybuf, y_hbm, sem_out, s - 1, 1 - slot):
            c.wait()
        for c in _out_copies(ybuf, y_hbm, sem_out, s, slot):
            c.wait()


def _sample_kernel(x_ref, h0_ref, prefix_ref, w_s_ref, w_g_ref, b_lo_ref, b_hi_ref, a_re_ref, a_im_ref,
                   c_lo_ref, c_hi_ref, d_ref, glu_w_ref, glu_b_ref, wp_lo_ref, wp_hi_ref, pscale_ref,
                   w_out_ref, ln_g_ref, ln_b_ref, y_ref, hnew_ref, pin_ref):
    x = x_ref[...]
    xb = x.astype(BF16)
    s_in, bu_lo, bu_hi = _ssm_input(xb, w_s_ref, b_lo_ref, b_hi_ref)
    bu = (bu_lo, bu_hi)
    for k in range(2):
        c_re = 2 * k * HALF_STATE
        c_im = c_re + HALF_STATE
        ar = a_re_ref[0:1, k * HALF_STATE:(k + 1) * HALF_STATE]
        ai = a_im_ref[0:1, k * HALF_STATE:(k + 1) * HALF_STATE]
        hr = h0_ref[:, c_re:c_re + HALF_STATE]
        hi = h0_ref[:, c_im:c_im + HALF_STATE]
        hnew_ref[:, c_re:c_re + HALF_STATE] = ar * hr - ai * hi + bu[k][:, :HALF_STATE]
        hnew_ref[:, c_im:c_im + HALF_STATE] = ar * hi + ai * hr + bu[k][:, HALF_STATE:]

    gates = jnp.dot(xb, w_g_ref[...], preferred_element_type=F32)
    p_in = gates[:, SSM_WIDTH:SSM_WIDTH + POOL_WIDTH]
    pin_ref[...] = p_in
    pooled = []
    for g, w in enumerate(POOL_WINDOWS):
        cols = slice(g * POOL_GROUP_CH, (g + 1) * POOL_GROUP_CH)
        acc = p_in[:, cols]
        for j in range(1, w):
            acc = acc + prefix_ref[POOL_BUF - j, :, cols]
        pooled.append(acc / float(w) - p_in[:, cols])
    pooled = jnp.concatenate(pooled, axis=-1)
    y_ref[...] = _mix_and_norm(x, s_in, hnew_ref[...].astype(BF16), gates, pooled, c_lo_ref, c_hi_ref, d_ref, glu_w_ref,
                               glu_b_ref, wp_lo_ref, wp_hi_ref, pscale_ref, w_out_ref, ln_g_ref, ln_b_ref)


def _group_mask(shape, row_group, col_group):
    r = lax.broadcasted_iota(jnp.int32, shape, 0) // row_group
    c = lax.broadcasted_iota(jnp.int32, shape, 1) // col_group
    return r == c


def _lane_tiler(width, copies):
    shape = (width, width * copies)
    r = lax.broadcasted_iota(jnp.int32, shape, 0)
    c = lax.broadcasted_iota(jnp.int32, shape, 1) % width
    return (r == c).astype(BF16)


def _prep_kernel(p_ref, b_re_ref, b_im_ref, c_re_ref, c_im_ref, pw_ref, w_in_ref, glu_w_ref, w_out_ref,
                 w_s16, w_g16, b_lo, b_hi, a_re, a_im, c_lo, c_hi, glu16, wp_lo, wp_hi, w_out16):
    w_s16[...] = w_in_ref[:, :SSM_WIDTH].astype(BF16)
    w_g16[...] = w_in_ref[:, SSM_WIDTH:].astype(BF16)
    glu16[...] = glu_w_ref[...].astype(BF16)
    w_out16[...] = w_out_ref[...].astype(BF16)

    lam_re = p_ref[0:1, :]
    lam_im = p_ref[1:2, :]
    dt = jnp.exp(p_ref[2:3, :])
    mag = jnp.exp(lam_re * dt)
    ar, ai = mag * jnp.cos(lam_im * dt), mag * jnp.sin(lam_im * dt)
    den = lam_re * lam_re + lam_im * lam_im
    nr = ar - 1.0
    qr = (nr * lam_re + ai * lam_im) / den
    qi = (ai * lam_re - nr * lam_im) / den
    n_cols = SSM_GROUPS * SSM_STATE
    a_re[...] = jnp.broadcast_to(ar, (SUBLANES, n_cols))
    a_im[...] = jnp.broadcast_to(ai, (SUBLANES, n_cols))

    tile_c = _lane_tiler(SSM_GROUP_CH, HALF_GROUPS)
    tile_n = _lane_tiler(SSM_STATE, HALF_GROUPS)
    mask_b = _group_mask((HALF_STATE, MXU_TILE), SSM_STATE, SSM_GROUP_CH)
    mask_c = _group_mask((MXU_TILE, HALF_STATE), SSM_GROUP_CH, SSM_STATE)
    for k, (b_out, c_out) in enumerate(((b_lo, c_lo), (b_hi, c_hi))):
        rows = slice(k * HALF_STATE, (k + 1) * HALF_STATE)
        q_re = jnp.broadcast_to(qr[:, rows], (128, HALF_STATE)).T[:, :SSM_GROUP_CH]
        q_im = jnp.broadcast_to(qi[:, rows], (128, HALF_STATE)).T[:, :SSM_GROUP_CH]
        b_re = b_re_ref[rows, :]
        b_im = b_im_ref[rows, :]
        bbar = (q_re * b_re - q_im * b_im, q_re * b_im + q_im * b_re)
        for j, bb in enumerate(bbar):
            wide = jnp.dot(bb.astype(BF16), tile_c, preferred_element_type=F32)
            blk = jnp.where(mask_b, wide, 0.0).T
            b_out[:, j * HALF_STATE:(j + 1) * HALF_STATE] = blk.astype(BF16)
        crow = slice(k * MXU_TILE, (k + 1) * MXU_TILE)
        for j, (c_ref, sign) in enumerate(((c_re_ref, 1.0), (c_im_ref, -1.0))):
            wide = jnp.dot(c_ref[crow, :].astype(BF16), tile_n, preferred_element_type=F32)
            blk = jnp.where(mask_c, sign * wide, 0.0).T
            c_out[j * HALF_STATE:(j + 1) * HALF_STATE, :] = blk.astype(BF16)

    mask_p = _group_mask((MXU_TILE, MXU_TILE), POOL_GROUP_CH, POOL_GROUP_CH)
    for k, wp in enumerate((wp_lo, wp_hi)):
        pw = pw_ref[k * MXU_TILE:(k + 1) * MXU_TILE, :]
        wp[...] = jnp.where(mask_p, jnp.concatenate([pw, pw], axis=1), 0.0).astype(BF16)


def _prepare_weights(w_in, lam_re, lam_im, log_dt, b_re, b_im, c_re, c_im, d, glu_w, glu_b, pool_w,
                     pool_scale, w_out, ln_g, ln_b):
    n_cols = SSM_GROUPS * SSM_STATE
    params = jnp.stack([lam_re.astype(F32).reshape(n_cols), lam_im.astype(F32).reshape(n_cols),
                        jnp.repeat(log_dt.astype(F32), SSM_STATE)])
    bf = lambda shape: jax.ShapeDtypeStruct(shape, BF16)
    out_shape = dict(w_s=bf((D_MODEL, SSM_WIDTH)), w_g=bf((D_MODEL, SSM_WIDTH + 2 * POOL_WIDTH)),
                     b_lo=bf((MXU_TILE, 2 * HALF_STATE)), b_hi=bf((MXU_TILE, 2 * HALF_STATE)),
                     a_re=jax.ShapeDtypeStruct((SUBLANES, n_cols), F32),
                     a_im=jax.ShapeDtypeStruct((SUBLANES, n_cols), F32),
                     c_lo=bf((2 * HALF_STATE, MXU_TILE)), c_hi=bf((2 * HALF_STATE, MXU_TILE)),
                     glu_w=bf((SSM_WIDTH, SSM_WIDTH)), wp_lo=bf((MXU_TILE, MXU_TILE)),
                     wp_hi=bf((MXU_TILE, MXU_TILE)), w_out=bf((D_MODEL, D_MODEL)))
    names = list(out_shape)
    outs = pl.pallas_call(
        _prep_kernel,
        out_shape=[out_shape[k] for k in names],
        compiler_params=pltpu.CompilerParams(vmem_limit_bytes=VMEM_LIMIT_BYTES),
        name="prepare_weights",
    )(params, b_re.astype(F32).reshape(n_cols, SSM_GROUP_CH), b_im.astype(F32).reshape(n_cols, SSM_GROUP_CH),
      c_re.astype(F32).reshape(SSM_WIDTH, SSM_STATE), c_im.astype(F32).reshape(SSM_WIDTH, SSM_STATE),
      pool_w.astype(F32).reshape(POOL_WIDTH, POOL_GROUP_CH), w_in.astype(F32), glu_w.astype(F32),
      w_out.astype(F32))
    w = dict(zip(names, outs))
    row = lambda v: v.astype(F32).reshape(1, -1)
    w.update(d=row(d), glu_b=row(glu_b), pscale=row(pool_scale), ln_g=row(ln_g), ln_b=row(ln_b))
    return w


_WEIGHT_ORDER = ('w_s', 'w_g', 'b_lo', 'b_hi', 'a_re', 'a_im', 'c_lo', 'c_hi', 'd', 'glu_w', 'glu_b',
                 'wp_lo', 'wp_hi', 'pscale', 'w_out', 'ln_g', 'ln_b')


def _to_state_cols(h_re, h_im):
    b = h_re.shape[0]
    re = h_re.reshape(b, 2, HALF_STATE)
    im = h_im.reshape(b, 2, HALF_STATE)
    return jnp.stack([re[:, 0], im[:, 0], re[:, 1], im[:, 1]], axis=1).reshape(b, STATE_COLS)


def _from_state_cols(h):
    b = h.shape[0]
    h4 = h.reshape(b, 4, HALF_STATE)
    re = jnp.stack([h4[:, 0], h4[:, 2]], axis=1).reshape(b, SSM_GROUPS, SSM_STATE)
    im = jnp.stack([h4[:, 1], h4[:, 3]], axis=1).reshape(b, SSM_GROUPS, SSM_STATE)
    return re, im


def _resident(shape):
    return pl.BlockSpec(shape, lambda s: (0,) * len(shape), pipeline_mode=pl.Buffered(1))


def _prompt_layer(x, w):
    bsz, seq, _ = x.shape
    assert bsz == SUBLANES and seq % CHUNK_TOKENS == 0 and seq >= 2 * CHUNK_TOKENS
    rows = CHUNK_TOKENS * SUBLANES
    halo = HALO_TOKENS * SUBLANES
    weights = [w[k] for k in _WEIGHT_ORDER]
    y, hfin, ptail = pl.pallas_call(
        _prompt_kernel,
        grid=(seq // CHUNK_TOKENS,),
        in_specs=[pl.BlockSpec(memory_space=pl.ANY)] + [_resident(a.shape) for a in weights],
        out_specs=[pl.BlockSpec(memory_space=pl.ANY),
                   pl.BlockSpec((SUBLANES, STATE_COLS), lambda s: (0, 0)),
                   pl.BlockSpec((halo, POOL_WIDTH), lambda s: (0, 0))],
        out_shape=[jax.ShapeDtypeStruct(x.shape, F32),
                   jax.ShapeDtypeStruct((SUBLANES, STATE_COLS), F32),
                   jax.ShapeDtypeStruct((halo, POOL_WIDTH), F32)],
        scratch_shapes=[pltpu.VMEM((2, CHUNK_TOKENS, SUBLANES, D_MODEL), F32),
                        pltpu.VMEM((2, CHUNK_TOKENS, SUBLANES, D_MODEL), F32),
                        pltpu.VMEM((rows, STATE_COLS), F32),
                        pltpu.VMEM((rows, STATE_COLS), BF16),
                        pltpu.VMEM((halo + rows, POOL_WIDTH), F32),
                        pltpu.SemaphoreType.DMA((2,)),
                        pltpu.SemaphoreType.DMA((2,))],
        compiler_params=pltpu.CompilerParams(dimension_semantics=("arbitrary",),
                                             vmem_limit_bytes=VMEM_LIMIT_BYTES),
        name="prompt_layer",
    )(x, *weights)
    h_re, h_im = _from_state_cols(hfin)
    buf = ptail.reshape(HALO_TOKENS, SUBLANES, POOL_WIDTH)[HALO_TOKENS - POOL_BUF:]
    return y, h_re, h_im, jnp.swapaxes(buf, 0, 1)


def _sample_layer(x, h_re, h_im, pool_prefix, w):
    bsz = x.shape[0]
    weights = [w[k] for k in _WEIGHT_ORDER]
    h0 = _to_state_cols(h_re.astype(F32), h_im.astype(F32))
    prefix = jnp.swapaxes(pool_prefix.astype(F32), 0, 1)
    y, hnew, p_in = pl.pallas_call(
        _sample_kernel,
        out_shape=[jax.ShapeDtypeStruct((bsz, D_MODEL), F32),
                   jax.ShapeDtypeStruct((bsz, STATE_COLS), F32),
                   jax.ShapeDtypeStruct((bsz, POOL_WIDTH), F32)],
        compiler_params=pltpu.CompilerParams(vmem_limit_bytes=VMEM_LIMIT_BYTES),
        name="sample_layer",
    )(x, h0, prefix, *weights)
    n_re, n_im = _from_state_cols(hnew)
    buf = jnp.concatenate([pool_prefix.astype(F32)[:, 1:], p_in[:, None, :]], axis=1)
    return y, n_re, n_im, buf


def kernel(x_prompt, x_sample, state_ssm_re, state_ssm_im, state_pool, w_in, ssm_lambda_re, ssm_lambda_im, ssm_log_dt, ssm_b_re, ssm_b_im, ssm_c_re, ssm_c_im, ssm_d, glu_w, glu_b, pool_w, pool_scale, w_out, ln_g, ln_b):
    out_dtype = x_prompt.dtype
    depth = w_in.shape[0]
    hp = x_prompt.astype(F32)
    hs = x_sample.astype(F32).reshape(x_sample.shape[0], D_MODEL)
    p_re, p_im, p_buf, s_re, s_im, s_buf = [], [], [], [], [], []
    for l in range(depth):
        w = _prepare_weights(w_in[l], ssm_lambda_re[l], ssm_lambda_im[l], ssm_log_dt[l], ssm_b_re[l],
                             ssm_b_im[l], ssm_c_re[l], ssm_c_im[l], ssm_d[l], glu_w[l], glu_b[l],
                             pool_w[l], pool_scale[l], w_out[l], ln_g[l], ln_b[l])
        hp, hr, hi, buf = _prompt_layer(hp, w)
        p_re.append(hr); p_im.append(hi); p_buf.append(buf)
        hs, hr, hi, buf = _sample_layer(hs, state_ssm_re[l], state_ssm_im[l], state_pool[l], w)
        s_re.append(hr); s_im.append(hi); s_buf.append(buf)
    return (hp.astype(out_dtype), hs.reshape(x_sample.shape).astype(out_dtype),
            jnp.stack(p_re), jnp.stack(p_im), jnp.stack(p_buf),
            jnp.stack(s_re), jnp.stack(s_im), jnp.stack(s_buf))
```

```python
import functools
import math

import jax
import jax.numpy as jnp
from jax import lax
from jax.experimental import pallas as pl
from jax.experimental.pallas import tpu as pltpu

D_MODEL = 1024
SSM_WIDTH = 512
POOL_WIDTH = 512
SSM_GROUPS = 32
SSM_GROUP_CH = 16
SSM_STATE = 64
POOL_WINDOWS = (2, 4, 8, 16)
POOL_GROUP_CH = 128
POOL_BUF = 15
DN_ALPHA = 2.0 ** 0.25
LN_EPS = 1e-5

SUBLANES = 8
MXU_TILE = 256
HALF_GROUPS = MXU_TILE // SSM_GROUP_CH
HALF_STATE = HALF_GROUPS * SSM_STATE
LANES = 128
STATE_COLS = 4 * HALF_STATE
LANE_BLOCKS = HALF_STATE // LANES
CHUNK_TOKENS = 64
TAIL_BLOCKS = 2
HALO_TOKENS = 16
VMEM_LIMIT_BYTES = 48 * 1024 * 1024

F32 = jnp.float32
BF16 = jnp.bfloat16


def _dot(a, b):
    return jnp.dot(a.astype(BF16), b, preferred_element_type=F32)


def _state_col(half, block, im):
    return half * 2 * HALF_STATE + block * 2 * LANES + im * LANES


def _ssm_input(xb, w_s_ref, b_lo_ref, b_hi_ref):
    s_in = jnp.dot(xb, w_s_ref[...], preferred_element_type=F32)
    sb = s_in.astype(BF16)
    bu_lo = jnp.dot(sb[:, :MXU_TILE], b_lo_ref[...], preferred_element_type=F32)
    bu_hi = jnp.dot(sb[:, MXU_TILE:], b_hi_ref[...], preferred_element_type=F32)
    return s_in, bu_lo, bu_hi


def _layer_norm(r, ln_g_ref, ln_b_ref):
    mu = jnp.mean(r, axis=-1, keepdims=True)
    rc = r - mu
    var = jnp.mean(rc * rc, axis=-1, keepdims=True)
    return rc * lax.rsqrt(var + LN_EPS) * ln_g_ref[...] + ln_b_ref[...]


def _mix_and_norm(x, xb, s_in, hb, s_gate, pooled, n_blocks, w_g_ref, c_lo_ref, c_hi_ref, d_ref, glu_w_ref,
                  glu_b_ref, wp_lo_ref, wp_hi_ref, pscale_ref, w_out_ref, ln_g_ref, ln_b_ref):
    blk = x.shape[0] // n_blocks
    rs = [slice(i * blk, (i + 1) * blk) for i in range(n_blocks)]
    sy = []
    for r in rs:
        y_lo = jnp.dot(hb[r, :2 * HALF_STATE], c_lo_ref[...], preferred_element_type=F32)
        y_hi = jnp.dot(hb[r, 2 * HALF_STATE:], c_hi_ref[...], preferred_element_type=F32)
        sy.append(jnp.concatenate([y_lo, y_hi], axis=-1) + d_ref[...] * s_in[r])
    p_gate = jnp.dot(xb, w_g_ref[:, SSM_WIDTH + POOL_WIDTH:], preferred_element_type=F32)
    sy = [jax.nn.gelu(v) for v in sy]
    glu = [_dot(v, glu_w_ref[...]) + glu_b_ref[...] for v in sy]
    pb = pooled.astype(BF16)
    py = [jnp.concatenate([jnp.dot(pb[r, :MXU_TILE], wp_lo_ref[...], preferred_element_type=F32),
                           jnp.dot(pb[r, MXU_TILE:], wp_hi_ref[...], preferred_element_type=F32)], axis=-1)
          * pscale_ref[...] for r in rs]
    mixed = [jnp.concatenate([sy[i] * jax.nn.sigmoid(glu[i]) * jax.nn.silu(s_gate[r]),
                              py[i] * jax.nn.silu(p_gate[r])], axis=-1) for i, r in enumerate(rs)]
    res = [DN_ALPHA * x[r] + _dot(mixed[i], w_out_ref[...]) for i, r in enumerate(rs)]
    return [_layer_norm(v, ln_g_ref, ln_b_ref) for v in res]


def _in_copies(x_hbm, xbuf, sem, chunk, slot):
    return [pltpu.make_async_copy(x_hbm.at[b, pl.ds(chunk * CHUNK_TOKENS, CHUNK_TOKENS), :],
                                  xbuf.at[slot, :, b, :], sem.at[slot]) for b in range(SUBLANES)]


def _out_copies(ybuf, y_hbm, sem, chunk, slot):
    return [pltpu.make_async_copy(ybuf.at[slot, :, b, :],
                                  y_hbm.at[b, pl.ds(chunk * CHUNK_TOKENS, CHUNK_TOKENS), :],
                                  sem.at[slot]) for b in range(SUBLANES)]


def _prompt_kernel(x_hbm, w_s_ref, w_g_ref, b_lo_ref, b_hi_ref, a_re_ref, a_im_ref, c_lo_ref, c_hi_ref,
                   d_ref, glu_w_ref, glu_b_ref, wp_lo_ref, wp_hi_ref, pscale_ref, w_out_ref, ln_g_ref,
                   ln_b_ref, y_hbm, hfin_ref, ptail_ref, xbuf, ybuf, hbuf, hb16, pbuf, sem_in, sem_out):
    s = pl.program_id(0)
    n = pl.num_programs(0)
    slot = lax.rem(s, 2)
    rows = CHUNK_TOKENS * SUBLANES
    halo = HALO_TOKENS * SUBLANES

    @pl.when(s == 0)
    def _():
        for c in _in_copies(x_hbm, xbuf, sem_in, 0, 0):
            c.start()
        hfin_ref[...] = jnp.zeros_like(hfin_ref)
        pbuf[0:halo, :] = jnp.zeros((halo, POOL_WIDTH), F32)

    @pl.when(s + 1 < n)
    def _():
        for c in _in_copies(x_hbm, xbuf, sem_in, s + 1, 1 - slot):
            c.start()

    for c in _in_copies(x_hbm, xbuf, sem_in, s, slot):
        c.wait()

    @pl.when(s >= 2)
    def _():
        for c in _out_copies(ybuf, y_hbm, sem_out, s - 2, slot):
            c.wait()

    x = xbuf[slot].reshape(rows, D_MODEL)
    xb = x.astype(BF16)
    s_in, bu_lo, bu_hi = _ssm_input(xb, w_s_ref, b_lo_ref, b_hi_ref)
    hbuf[:, :2 * HALF_STATE] = bu_lo
    hbuf[:, 2 * HALF_STATE:] = bu_hi

    p_in = jnp.dot(xb, w_g_ref[:, SSM_WIDTH:SSM_WIDTH + POOL_WIDTH], preferred_element_type=F32)
    s_gate = jnp.dot(xb, w_g_ref[:, :SSM_WIDTH], preferred_element_type=F32)
    pbuf[halo:halo + rows, :] = p_in

    for k in range(2):
        for p in range(LANE_BLOCKS):
            c_re, c_im = _state_col(k, p, 0), _state_col(k, p, 1)
            a_col = k * HALF_STATE + p * LANES
            ar = a_re_ref[:, a_col:a_col + LANES]
            ai = a_im_ref[:, a_col:a_col + LANES]
            hr = hfin_ref[:, c_re:c_re + LANES]
            hi = hfin_ref[:, c_im:c_im + LANES]
            for t2 in range(CHUNK_TOKENS // 2):
                pair = []
                for t in (2 * t2, 2 * t2 + 1):
                    r = t * SUBLANES
                    hr, hi = (ar * hr - ai * hi + hbuf[r:r + SUBLANES, c_re:c_re + LANES],
                              ar * hi + ai * hr + hbuf[r:r + SUBLANES, c_im:c_im + LANES])
                    pair.append((hr, hi))
                r2 = 2 * t2 * SUBLANES
                hb16[r2:r2 + 2 * SUBLANES, c_re:c_re + LANES] = jnp.concatenate(
                    [pair[0][0], pair[1][0]], axis=0).astype(BF16)
                hb16[r2:r2 + 2 * SUBLANES, c_im:c_im + LANES] = jnp.concatenate(
                    [pair[0][1], pair[1][1]], axis=0).astype(BF16)
            hfin_ref[:, c_re:c_re + LANES] = hr
            hfin_ref[:, c_im:c_im + LANES] = hi

    row = lax.broadcasted_iota(jnp.int32, (rows, POOL_GROUP_CH), 0)
    pos1 = s * CHUNK_TOKENS + lax.shift_right_logical(row, 3) + 1
    pooled = []
    for g, w in enumerate(POOL_WINDOWS):
        acc = pbuf[:, g * POOL_GROUP_CH:(g + 1) * POOL_GROUP_CH]
        shift = SUBLANES
        while shift < w * SUBLANES:
            acc = acc[shift:, :] + acc[:-shift, :]
            shift *= 2
        acc = acc[acc.shape[0] - rows:, :]
        cnt = jnp.minimum(pos1, w).astype(F32)
        pooled.append(acc / cnt - p_in[:, g * POOL_GROUP_CH:(g + 1) * POOL_GROUP_CH])
    pooled = jnp.concatenate(pooled, axis=-1)
    pbuf[0:halo, :] = pbuf[rows:rows + halo, :]
    ptail_ref[...] = pbuf[0:halo, :]

    blk_tokens = CHUNK_TOKENS // TAIL_BLOCKS
    ys = _mix_and_norm(x, xb, s_in, hb16, s_gate, pooled, TAIL_BLOCKS, w_g_ref, c_lo_ref, c_hi_ref, d_ref,
                       glu_w_ref, glu_b_ref, wp_lo_ref, wp_hi_ref, pscale_ref, w_out_ref, ln_g_ref, ln_b_ref)
    for i, y in enumerate(ys):
        ybuf[slot, i * blk_tokens:(i + 1) * blk_tokens] = y.reshape(blk_tokens, SUBLANES, D_MODEL)

    for c in _out_copies(ybuf, y_hbm, sem_out, s, slot):
        c.start()

    @pl.when(s == n - 1)
    def _():
        for c in _out_copies(ybuf, y_hbm, sem_out, s - 1, 1 - slot):
            c.wait()
        for c in _out_copies(ybuf, y_hbm, sem_out, s, slot):
            c.wait()


def _sample_kernel(x_ref, h0_ref, prefix_ref, w_s_ref, w_g_ref, b_lo_ref, b_hi_ref, a_re_ref, a_im_ref,
                   c_lo_ref, c_hi_ref, d_ref, glu_w_ref, glu_b_ref, wp_lo_ref, wp_hi_ref, pscale_ref,
                   w_out_ref, ln_g_ref, ln_b_ref, y_ref, hnew_ref, pin_ref):
    x = x_ref[...]
    xb = x.astype(BF16)
    s_in, bu_lo, bu_hi = _ssm_input(xb, w_s_ref, b_lo_ref, b_hi_ref)
    bu = (bu_lo, bu_hi)
    for k in range(2):
        for p in range(LANE_BLOCKS):
            c_re, c_im = _state_col(k, p, 0), _state_col(k, p, 1)
            a_col = k * HALF_STATE + p * LANES
            ar = a_re_ref[0:1, a_col:a_col + LANES]
            ai = a_im_ref[0:1, a_col:a_col + LANES]
            hr = h0_ref[:, c_re:c_re + LANES]
            hi = h0_ref[:, c_im:c_im + LANES]
            b_re, b_im = _state_col(0, p, 0), _state_col(0, p, 1)
            hnew_ref[:, c_re:c_re + LANES] = ar * hr - ai * hi + bu[k][:, b_re:b_re + LANES]
            hnew_ref[:, c_im:c_im + LANES] = ar * hi + ai * hr + bu[k][:, b_im:b_im + LANES]

    p_in = jnp.dot(xb, w_g_ref[:, SSM_WIDTH:SSM_WIDTH + POOL_WIDTH], preferred_element_type=F32)
    s_gate = jnp.dot(xb, w_g_ref[:, :SSM_WIDTH], preferred_element_type=F32)
    pin_ref[...] = p_in
    pooled = []
    for g, w in enumerate(POOL_WINDOWS):
        cols = slice(g * POOL_GROUP_CH, (g + 1) * POOL_GROUP_CH)
        acc = p_in[:, cols]
        for j in range(1, w):
            acc = acc + prefix_ref[POOL_BUF - j, :, cols]
        pooled.append(acc / float(w) - p_in[:, cols])
    pooled = jnp.concatenate(pooled, axis=-1)
    y_ref[...] = _mix_and_norm(x, xb, s_in, hnew_ref[...].astype(BF16), s_gate, pooled, 1, w_g_ref, c_lo_ref,
                               c_hi_ref, d_ref, glu_w_ref, glu_b_ref, wp_lo_ref, wp_hi_ref, pscale_ref, w_out_ref,
                               ln_g_ref, ln_b_ref)[0]


def _group_mask(shape, row_group, col_group):
    r = lax.broadcasted_iota(jnp.int32, shape, 0) // row_group
    c = lax.broadcasted_iota(jnp.int32, shape, 1) // col_group
    return r == c


def _lane_tiler(width, copies):
    shape = (width, width * copies)
    r = lax.broadcasted_iota(jnp.int32, shape, 0)
    c = lax.broadcasted_iota(jnp.int32, shape, 1) % width
    return (r == c).astype(BF16)


def _prep_kernel(p_ref, b_re_ref, b_im_ref, c_re_ref, c_im_ref, pw_ref, w_in_ref, glu_w_ref, w_out_ref,
                 w_s16, w_g16, b_lo, b_hi, a_re, a_im, c_lo, c_hi, glu16, wp_lo, wp_hi, w_out16):
    w_s16[...] = w_in_ref[:, :SSM_WIDTH].astype(BF16)
    w_g16[...] = w_in_ref[:, SSM_WIDTH:].astype(BF16)
    glu16[...] = glu_w_ref[...].astype(BF16)
    w_out16[...] = w_out_ref[...].astype(BF16)

    lam_re = p_ref[0:1, :]
    lam_im = p_ref[1:2, :]
    dt = jnp.exp(p_ref[2:3, :])
    mag = jnp.exp(lam_re * dt)
    ar, ai = mag * jnp.cos(lam_im * dt), mag * jnp.sin(lam_im * dt)
    den = lam_re * lam_re + lam_im * lam_im
    nr = ar - 1.0
    qr = (nr * lam_re + ai * lam_im) / den
    qi = (ai * lam_re - nr * lam_im) / den
    n_cols = SSM_GROUPS * SSM_STATE
    a_re[...] = jnp.broadcast_to(ar, (SUBLANES, n_cols))
    a_im[...] = jnp.broadcast_to(ai, (SUBLANES, n_cols))

    tile_c = _lane_tiler(SSM_GROUP_CH, HALF_GROUPS)
    tile_n = _lane_tiler(SSM_STATE, HALF_GROUPS)
    mask_b = _group_mask((HALF_STATE, MXU_TILE), SSM_STATE, SSM_GROUP_CH)
    mask_c = _group_mask((MXU_TILE, HALF_STATE), SSM_GROUP_CH, SSM_STATE)
    for k, (b_out, c_out) in enumerate(((b_lo, c_lo), (b_hi, c_hi))):
        rows = slice(k * HALF_STATE, (k + 1) * HALF_STATE)
        q_re = jnp.broadcast_to(qr[:, rows], (128, HALF_STATE)).T[:, :SSM_GROUP_CH]
        q_im = jnp.broadcast_to(qi[:, rows], (128, HALF_STATE)).T[:, :SSM_GROUP_CH]
        b_re = b_re_ref[rows, :]
        b_im = b_im_ref[rows, :]
        bbar = (q_re * b_re - q_im * b_im, q_re * b_im + q_im * b_re)
        for j, bb in enumerate(bbar):
            wide = jnp.dot(bb.astype(BF16), tile_c, preferred_element_type=F32)
            blk = jnp.where(mask_b, wide, 0.0).T.astype(BF16)
            for p in range(LANE_BLOCKS):
                col = _state_col(0, p, j)
                b_out[:, col:col + LANES] = blk[:, p * LANES:(p + 1) * LANES]
        crow = slice(k * MXU_TILE, (k + 1) * MXU_TILE)
        for j, (c_ref, sign) in enumerate(((c_re_ref, 1.0), (c_im_ref, -1.0))):
            wide = jnp.dot(c_ref[crow, :].astype(BF16), tile_n, preferred_element_type=F32)
            blk = jnp.where(mask_c, sign * wide, 0.0).T.astype(BF16)
            for p in range(LANE_BLOCKS):
                row = _state_col(0, p, j)
                c_out[row:row + LANES, :] = blk[p * LANES:(p + 1) * LANES, :]

    mask_p = _group_mask((MXU_TILE, MXU_TILE), POOL_GROUP_CH, POOL_GROUP_CH)
    for k, wp in enumerate((wp_lo, wp_hi)):
        pw = pw_ref[k * MXU_TILE:(k + 1) * MXU_TILE, :]
        wp[...] = jnp.where(mask_p, jnp.concatenate([pw, pw], axis=1), 0.0).astype(BF16)


def _prepare_weights(w_in, lam_re, lam_im, log_dt, b_re, b_im, c_re, c_im, d, glu_w, glu_b, pool_w,
                     pool_scale, w_out, ln_g, ln_b):
    n_cols = SSM_GROUPS * SSM_STATE
    params = jnp.stack([lam_re.astype(F32).reshape(n_cols), lam_im.astype(F32).reshape(n_cols),
                        jnp.repeat(log_dt.astype(F32), SSM_STATE)])
    bf = lambda shape: jax.ShapeDtypeStruct(shape, BF16)
    out_shape = dict(w_s=bf((D_MODEL, SSM_WIDTH)), w_g=bf((D_MODEL, SSM_WIDTH + 2 * POOL_WIDTH)),
                     b_lo=bf((MXU_TILE, 2 * HALF_STATE)), b_hi=bf((MXU_TILE, 2 * HALF_STATE)),
                     a_re=jax.ShapeDtypeStruct((SUBLANES, n_cols), F32),
                     a_im=jax.ShapeDtypeStruct((SUBLANES, n_cols), F32),
                     c_lo=bf((2 * HALF_STATE, MXU_TILE)), c_hi=bf((2 * HALF_STATE, MXU_TILE)),
                     glu_w=bf((SSM_WIDTH, SSM_WIDTH)), wp_lo=bf((MXU_TILE, MXU_TILE)),
                     wp_hi=bf((MXU_TILE, MXU_TILE)), w_out=bf((D_MODEL, D_MODEL)))
    names = list(out_shape)
    outs = pl.pallas_call(
        _prep_kernel,
        out_shape=[out_shape[k] for k in names],
        compiler_params=pltpu.CompilerParams(vmem_limit_bytes=VMEM_LIMIT_BYTES),
        name="prepare_weights",
    )(params, b_re.astype(F32).reshape(n_cols, SSM_GROUP_CH), b_im.astype(F32).reshape(n_cols, SSM_GROUP_CH),
      c_re.astype(F32).reshape(SSM_WIDTH, SSM_STATE), c_im.astype(F32).reshape(SSM_WIDTH, SSM_STATE),
      pool_w.astype(F32).reshape(POOL_WIDTH, POOL_GROUP_CH), w_in.astype(F32), glu_w.astype(F32),
      w_out.astype(F32))
    w = dict(zip(names, outs))
    row = lambda v: v.astype(F32).reshape(1, -1)
    w.update(d=row(d), glu_b=row(glu_b), pscale=row(pool_scale), ln_g=row(ln_g), ln_b=row(ln_b))
    return w


_WEIGHT_ORDER = ('w_s', 'w_g', 'b_lo', 'b_hi', 'a_re', 'a_im', 'c_lo', 'c_hi', 'd', 'glu_w', 'glu_b',
                 'wp_lo', 'wp_hi', 'pscale', 'w_out', 'ln_g', 'ln_b')


def _to_state_cols(h_re, h_im):
    b = h_re.shape[0]
    re = h_re.reshape(b, 2 * LANE_BLOCKS, 1, LANES)
    im = h_im.reshape(b, 2 * LANE_BLOCKS, 1, LANES)
    return jnp.concatenate([re, im], axis=2).reshape(b, STATE_COLS)


def _from_state_cols(h):
    b = h.shape[0]
    h4 = h.reshape(b, 2 * LANE_BLOCKS, 2, LANES)
    return (h4[:, :, 0].reshape(b, SSM_GROUPS, SSM_STATE), h4[:, :, 1].reshape(b, SSM_GROUPS, SSM_STATE))


def _resident(shape):
    return pl.BlockSpec(shape, lambda s: (0,) * len(shape), pipeline_mode=pl.Buffered(1))


def _prompt_layer(x, w):
    bsz, seq, _ = x.shape
    assert bsz == SUBLANES and seq % CHUNK_TOKENS == 0 and seq >= 2 * CHUNK_TOKENS
    rows = CHUNK_TOKENS * SUBLANES
    halo = HALO_TOKENS * SUBLANES
    weights = [w[k] for k in _WEIGHT_ORDER]
    y, hfin, ptail = pl.pallas_call(
        _prompt_kernel,
        grid=(seq // CHUNK_TOKENS,),
        in_specs=[pl.BlockSpec(memory_space=pl.ANY)] + [_resident(a.shape) for a in weights],
        out_specs=[pl.BlockSpec(memory_space=pl.ANY),
                   pl.BlockSpec((SUBLANES, STATE_COLS), lambda s: (0, 0)),
                   pl.BlockSpec((halo, POOL_WIDTH), lambda s: (0, 0))],
        out_shape=[jax.ShapeDtypeStruct(x.shape, F32),
                   jax.ShapeDtypeStruct((SUBLANES, STATE_COLS), F32),
                   jax.ShapeDtypeStruct((halo, POOL_WIDTH), F32)],
        scratch_shapes=[pltpu.VMEM((2, CHUNK_TOKENS, SUBLANES, D_MODEL), F32),
                        pltpu.VMEM((2, CHUNK_TOKENS, SUBLANES, D_MODEL), F32),
                        pltpu.VMEM((rows, STATE_COLS), F32),
                        pltpu.VMEM((rows, STATE_COLS), BF16),
                        pltpu.VMEM((halo + rows, POOL_WIDTH), F32),
                        pltpu.SemaphoreType.DMA((2,)),
                        pltpu.SemaphoreType.DMA((2,))],
        compiler_params=pltpu.CompilerParams(dimension_semantics=("arbitrary",),
                                             vmem_limit_bytes=VMEM_LIMIT_BYTES),
        name="prompt_layer",
    )(x, *weights)
    h_re, h_im = _from_state_cols(hfin)
    buf = ptail.reshape(HALO_TOKENS, SUBLANES, POOL_WIDTH)[HALO_TOKENS - POOL_BUF:]
    return y, h_re, h_im, jnp.swapaxes(buf, 0, 1)


def _sample_layer(x, h_re, h_im, pool_prefix, w):
    bsz = x.shape[0]
    weights = [w[k] for k in _WEIGHT_ORDER]
    h0 = _to_state_cols(h_re.astype(F32), h_im.astype(F32))
    prefix = jnp.swapaxes(pool_prefix.astype(F32), 0, 1)
    y, hnew, p_in = pl.pallas_call(
        _sample_kernel,
        out_shape=[jax.ShapeDtypeStruct((bsz, D_MODEL), F32),
                   jax.ShapeDtypeStruct((bsz, STATE_COLS), F32),
                   jax.ShapeDtypeStruct((bsz, POOL_WIDTH), F32)],
        compiler_params=pltpu.CompilerParams(vmem_limit_bytes=VMEM_LIMIT_BYTES),
        name="sample_layer",
    )(x, h0, prefix, *weights)
    n_re, n_im = _from_state_cols(hnew)
    buf = jnp.concatenate([pool_prefix.astype(F32)[:, 1:], p_in[:, None, :]], axis=1)
    return y, n_re, n_im, buf


def kernel(x_prompt, x_sample, state_ssm_re, state_ssm_im, state_pool, w_in, ssm_lambda_re, ssm_lambda_im, ssm_log_dt, ssm_b_re, ssm_b_im, ssm_c_re, ssm_c_im, ssm_d, glu_w, glu_b, pool_w, pool_scale, w_out, ln_g, ln_b):
    out_dtype = x_prompt.dtype
    depth = w_in.shape[0]
    hp = x_prompt.astype(F32)
    hs = x_sample.astype(F32).reshape(x_sample.shape[0], D_MODEL)
    p_re, p_im, p_buf, s_re, s_im, s_buf = [], [], [], [], [], []
    for l in range(depth):
        w = _prepare_weights(w_in[l], ssm_lambda_re[l], ssm_lambda_im[l], ssm_log_dt[l], ssm_b_re[l],
                             ssm_b_im[l], ssm_c_re[l], ssm_c_im[l], ssm_d[l], glu_w[l], glu_b[l],
                             pool_w[l], pool_scale[l], w_out[l], ln_g[l], ln_b[l])
        hp, hr, hi, buf = _prompt_layer(hp, w)
        p_re.append(hr); p_im.append(hi); p_buf.append(buf)
        hs, hr, hi, buf = _sample_layer(hs, state_ssm_re[l], state_ssm_im[l], state_pool[l], w)
        s_re.append(hr); s_im.append(hi); s_buf.append(buf)
    return (hp.astype(out_dtype), hs.reshape(x_sample.shape).astype(out_dtype),
            jnp.stack(p_re), jnp.stack(p_im), jnp.stack(p_buf),
            jnp.stack(s_re), jnp.stack(s_im), jnp.stack(s_buf))
```

```python
import jax
import jax.numpy as jnp
from jax import lax
from jax.experimental import pallas as pl
from jax.experimental.pallas import tpu as pltpu

D_MODEL = 1024
SSM_WIDTH = 512
POOL_WIDTH = 512
SSM_GROUPS = 32
SSM_GROUP_CH = 16
SSM_STATE = 64
POOL_WINDOWS = (2, 4, 8, 16)
POOL_GROUP_CH = 128
POOL_BUF = 15
DN_ALPHA = 2.0 ** 0.25
LN_EPS = 1e-5

SUBLANES = 8
MXU_TILE = 256
HALF_GROUPS = MXU_TILE // SSM_GROUP_CH
HALF_STATE = HALF_GROUPS * SSM_STATE
LANES = 128
STATE_COLS = 4 * HALF_STATE
LANE_BLOCKS = HALF_STATE // LANES
CHUNK_TOKENS = 64
TAIL_BLOCKS = 2
Y_SLOTS = 3
HALO_TOKENS = 16
VMEM_LIMIT_BYTES = 48 * 1024 * 1024

F32 = jnp.float32
BF16 = jnp.bfloat16


def _dot(a, b):
    return jnp.dot(a.astype(BF16), b, preferred_element_type=F32)


def _state_col(half, block, im):
    return half * 2 * HALF_STATE + block * 2 * LANES + im * LANES


def _ssm_input(xb, w_s_ref, b_lo_ref, b_hi_ref):
    s_in = jnp.dot(xb, w_s_ref[...], preferred_element_type=F32)
    sb = s_in.astype(BF16)
    bu_lo = jnp.dot(sb[:, :MXU_TILE], b_lo_ref[...], preferred_element_type=F32)
    bu_hi = jnp.dot(sb[:, MXU_TILE:], b_hi_ref[...], preferred_element_type=F32)
    return s_in, bu_lo, bu_hi


def _layer_norm(r, ln_g_ref, ln_b_ref):
    mu = jnp.mean(r, axis=-1, keepdims=True)
    rc = r - mu
    var = jnp.mean(rc * rc, axis=-1, keepdims=True)
    return rc * lax.rsqrt(var + LN_EPS) * ln_g_ref[...] + ln_b_ref[...]


def _mix(x, xb, s_in, hb, pooled, n_blocks, w_g_ref, c_lo_ref, c_hi_ref, d_ref, glu_w_ref,
         glu_b_ref, wp_lo_ref, wp_hi_ref, pscale_ref, w_out_ref):
    blk = x.shape[0] // n_blocks
    rs = [slice(i * blk, (i + 1) * blk) for i in range(n_blocks)]
    sy = []
    for r in rs:
        y_lo = jnp.dot(hb[r, :2 * HALF_STATE], c_lo_ref[...], preferred_element_type=F32)
        y_hi = jnp.dot(hb[r, 2 * HALF_STATE:], c_hi_ref[...], preferred_element_type=F32)
        sy.append(jnp.concatenate([y_lo, y_hi], axis=-1) + d_ref[...] * s_in[r])
    s_gate = jnp.dot(xb, w_g_ref[:, :SSM_WIDTH], preferred_element_type=F32)
    sy = [jax.nn.gelu(v) for v in sy]
    glu = [_dot(v, glu_w_ref[...]) + glu_b_ref[...] for v in sy]
    p_gate = jnp.dot(xb, w_g_ref[:, SSM_WIDTH + POOL_WIDTH:], preferred_element_type=F32)
    pb = pooled.astype(BF16)
    py = [jnp.concatenate([jnp.dot(pb[r, :MXU_TILE], wp_lo_ref[...], preferred_element_type=F32),
                           jnp.dot(pb[r, MXU_TILE:], wp_hi_ref[...], preferred_element_type=F32)], axis=-1)
          * pscale_ref[...] for r in rs]
    mixed = [jnp.concatenate([sy[i] * jax.nn.sigmoid(glu[i]) * jax.nn.silu(s_gate[r]),
                              py[i] * jax.nn.silu(p_gate[r])], axis=-1) for i, r in enumerate(rs)]
    return [DN_ALPHA * x[r] + _dot(mixed[i], w_out_ref[...]) for i, r in enumerate(rs)]


def _in_copies(x_hbm, xbuf, sem, chunk, slot):
    return [pltpu.make_async_copy(x_hbm.at[b, pl.ds(chunk * CHUNK_TOKENS, CHUNK_TOKENS), :],
                                  xbuf.at[slot, :, b, :], sem.at[slot]) for b in range(SUBLANES)]


def _out_copies(ybuf, y_hbm, sem, chunk, slot):
    return [pltpu.make_async_copy(ybuf.at[slot, :, b, :],
                                  y_hbm.at[b, pl.ds(chunk * CHUNK_TOKENS, CHUNK_TOKENS), :],
                                  sem.at[slot]) for b in range(SUBLANES)]


def _prompt_kernel(x_hbm, w_s_ref, w_g_ref, b_lo_ref, b_hi_ref, a_re_ref, a_im_ref, c_lo_ref, c_hi_ref,
                   d_ref, glu_w_ref, glu_b_ref, wp_lo_ref, wp_hi_ref, pscale_ref, w_out_ref, ln_g_ref,
                   ln_b_ref, y_hbm, hre_ref, him_ref, ptail_ref, xbuf, ybuf, hbuf, hb16, pbuf, sem_in, sem_out):
    s = pl.program_id(0)
    n_chunks = pl.num_programs(0) - 1
    xslot = lax.rem(s, 2)
    yslot = lax.rem(s, Y_SLOTS)
    yslot_prev = lax.rem(s + Y_SLOTS - 1, Y_SLOTS)
    yslot_prev2 = lax.rem(s + Y_SLOTS - 2, Y_SLOTS)
    rows = CHUNK_TOKENS * SUBLANES
    halo = HALO_TOKENS * SUBLANES

    def finish_prev():
        r = ybuf[yslot_prev].reshape(rows, D_MODEL)
        ybuf[yslot_prev] = _layer_norm(r, ln_g_ref, ln_b_ref).reshape(CHUNK_TOKENS, SUBLANES, D_MODEL)

    @pl.when(s == 0)
    def _():
        for c in _in_copies(x_hbm, xbuf, sem_in, 0, 0):
            c.start()
        hre_ref[...] = jnp.zeros_like(hre_ref)
        him_ref[...] = jnp.zeros_like(him_ref)
        pbuf[0:halo, :] = jnp.zeros((halo, POOL_WIDTH), F32)
        ybuf[Y_SLOTS - 1] = jnp.zeros((CHUNK_TOKENS, SUBLANES, D_MODEL), F32)

    @pl.when(s + 1 < n_chunks)
    def _():
        for c in _in_copies(x_hbm, xbuf, sem_in, s + 1, 1 - xslot):
            c.start()

    @pl.when(s < n_chunks)
    def _():
        for c in _in_copies(x_hbm, xbuf, sem_in, s, xslot):
            c.wait()
        finish_prev()

        x = xbuf[xslot].reshape(rows, D_MODEL)
        xb = x.astype(BF16)
        s_in, bu_lo, bu_hi = _ssm_input(xb, w_s_ref, b_lo_ref, b_hi_ref)
        hbuf[:, :2 * HALF_STATE] = bu_lo
        hbuf[:, 2 * HALF_STATE:] = bu_hi

        p_in = jnp.dot(xb, w_g_ref[:, SSM_WIDTH:SSM_WIDTH + POOL_WIDTH], preferred_element_type=F32)
        pbuf[halo:halo + rows, :] = p_in

        for k in range(2):
            for p in range(LANE_BLOCKS):
                c_re, c_im = _state_col(k, p, 0), _state_col(k, p, 1)
                a_col = k * HALF_STATE + p * LANES
                ar = a_re_ref[:, a_col:a_col + LANES]
                ai = a_im_ref[:, a_col:a_col + LANES]
                hr = hre_ref[:, a_col:a_col + LANES]
                hi = him_ref[:, a_col:a_col + LANES]
                for t2 in range(CHUNK_TOKENS // 2):
                    pair = []
                    for t in (2 * t2, 2 * t2 + 1):
                        r = t * SUBLANES
                        hr, hi = (ar * hr - ai * hi + hbuf[r:r + SUBLANES, c_re:c_re + LANES],
                                  ar * hi + ai * hr + hbuf[r:r + SUBLANES, c_im:c_im + LANES])
                        pair.append((hr, hi))
                    r2 = 2 * t2 * SUBLANES
                    hb16[r2:r2 + 2 * SUBLANES, c_re:c_re + LANES] = jnp.concatenate(
                        [pair[0][0], pair[1][0]], axis=0).astype(BF16)
                    hb16[r2:r2 + 2 * SUBLANES, c_im:c_im + LANES] = jnp.concatenate(
                        [pair[0][1], pair[1][1]], axis=0).astype(BF16)
                hre_ref[:, a_col:a_col + LANES] = hr
                him_ref[:, a_col:a_col + LANES] = hi

        row = lax.broadcasted_iota(jnp.int32, (rows, POOL_GROUP_CH), 0)
        pos1 = s * CHUNK_TOKENS + lax.shift_right_logical(row, 3) + 1
        pooled = []
        for g, w in enumerate(POOL_WINDOWS):
            acc = pbuf[:, g * POOL_GROUP_CH:(g + 1) * POOL_GROUP_CH]
            shift = SUBLANES
            while shift < w * SUBLANES:
                acc = acc[shift:, :] + acc[:-shift, :]
                shift *= 2
            acc = acc[acc.shape[0] - rows:, :]
            cnt = jnp.minimum(pos1, w).astype(F32)
            pooled.append(acc / cnt - p_in[:, g * POOL_GROUP_CH:(g + 1) * POOL_GROUP_CH])
        pooled = jnp.concatenate(pooled, axis=-1)
        pbuf[0:halo, :] = pbuf[rows:rows + halo, :]
        ptail_ref[...] = pbuf[0:halo, :]

        blk_tokens = CHUNK_TOKENS // TAIL_BLOCKS
        res = _mix(x, xb, s_in, hb16, pooled, TAIL_BLOCKS, w_g_ref, c_lo_ref, c_hi_ref, d_ref,
                   glu_w_ref, glu_b_ref, wp_lo_ref, wp_hi_ref, pscale_ref, w_out_ref)
        for i, r in enumerate(res):
            ybuf[yslot, i * blk_tokens:(i + 1) * blk_tokens] = r.reshape(blk_tokens, SUBLANES, D_MODEL)

    @pl.when(s == n_chunks)
    def _():
        finish_prev()

    @pl.when(s >= 1)
    def _():
        for c in _out_copies(ybuf, y_hbm, sem_out, s - 1, yslot_prev):
            c.start()

    @pl.when(s >= 2)
    def _():
        for c in _out_copies(ybuf, y_hbm, sem_out, s - 2, yslot_prev2):
            c.wait()

    @pl.when(s == n_chunks)
    def _():
        for c in _out_copies(ybuf, y_hbm, sem_out, s - 1, yslot_prev):
            c.wait()


def _sample_kernel(x_hbm, h0_re_ref, h0_im_ref, prefix_ref, w_s_ref, w_g_ref, b_lo_ref, b_hi_ref, a_re_ref,
                   a_im_ref, c_lo_ref, c_hi_ref, d_ref, glu_w_ref, glu_b_ref, wp_lo_ref, wp_hi_ref, pscale_ref,
                   w_out_ref, ln_g_ref, ln_b_ref, y_hbm, hre_ref, him_ref, pin_ref, xs, ys, hb16, sem):
    load = pltpu.make_async_copy(x_hbm.at[:, 0, :], xs, sem.at[0])
    load.start()
    load.wait()
    x = xs[...]
    xb = x.astype(BF16)
    s_in, bu_lo, bu_hi = _ssm_input(xb, w_s_ref, b_lo_ref, b_hi_ref)
    bu = (bu_lo, bu_hi)
    for k in range(2):
        for p in range(LANE_BLOCKS):
            c_re, c_im = _state_col(k, p, 0), _state_col(k, p, 1)
            a_col = k * HALF_STATE + p * LANES
            ar = a_re_ref[0:1, a_col:a_col + LANES]
            ai = a_im_ref[0:1, a_col:a_col + LANES]
            hr = h0_re_ref[:, a_col:a_col + LANES]
            hi = h0_im_ref[:, a_col:a_col + LANES]
            b_re, b_im = _state_col(0, p, 0), _state_col(0, p, 1)
            nhr = ar * hr - ai * hi + bu[k][:, b_re:b_re + LANES]
            nhi = ar * hi + ai * hr + bu[k][:, b_im:b_im + LANES]
            hre_ref[:, a_col:a_col + LANES] = nhr
            him_ref[:, a_col:a_col + LANES] = nhi
            hb16[:, c_re:c_re + LANES] = nhr.astype(BF16)
            hb16[:, c_im:c_im + LANES] = nhi.astype(BF16)

    p_in = jnp.dot(xb, w_g_ref[:, SSM_WIDTH:SSM_WIDTH + POOL_WIDTH], preferred_element_type=F32)
    pin_ref[...] = p_in
    pooled = []
    for g, w in enumerate(POOL_WINDOWS):
        cols = slice(g * POOL_GROUP_CH, (g + 1) * POOL_GROUP_CH)
        acc = p_in[:, cols]
        for j in range(1, w):
            acc = acc + prefix_ref[POOL_BUF - j, :, cols]
        pooled.append(acc / float(w) - p_in[:, cols])
    pooled = jnp.concatenate(pooled, axis=-1)
    res = _mix(x, xb, s_in, hb16, pooled, 1, w_g_ref, c_lo_ref, c_hi_ref, d_ref, glu_w_ref, glu_b_ref,
               wp_lo_ref, wp_hi_ref, pscale_ref, w_out_ref)[0]
    ys[...] = _layer_norm(res, ln_g_ref, ln_b_ref)
    store = pltpu.make_async_copy(ys, y_hbm.at[:, 0, :], sem.at[1])
    store.start()
    store.wait()


def _group_mask(shape, row_group, col_group):
    r = lax.broadcasted_iota(jnp.int32, shape, 0) // row_group
    c = lax.broadcasted_iota(jnp.int32, shape, 1) // col_group
    return r == c


def _lane_tiler(width, copies):
    shape = (width, width * copies)
    r = lax.broadcasted_iota(jnp.int32, shape, 0)
    c = lax.broadcasted_iota(jnp.int32, shape, 1) % width
    return (r == c).astype(BF16)


def _prep_kernel(lam_re_ref, lam_im_ref, log_dt_ref, b_re_ref, b_im_ref, c_re_ref, c_im_ref, pw_ref, w_in_ref,
                 glu_w_ref, w_out_ref,
                 w_s16, w_g16, b_lo, b_hi, a_re, a_im, c_lo, c_hi, glu16, wp_lo, wp_hi, w_out16):
    w_s16[...] = w_in_ref[:, :SSM_WIDTH].astype(BF16)
    w_g16[...] = w_in_ref[:, SSM_WIDTH:].astype(BF16)
    glu16[...] = glu_w_ref[...].astype(BF16)
    w_out16[...] = w_out_ref[...].astype(BF16)

    lam_re = lam_re_ref[...]
    lam_im = lam_im_ref[...]
    dt = jnp.exp(log_dt_ref[...])
    mag = jnp.exp(lam_re * dt)
    ar, ai = mag * jnp.cos(lam_im * dt), mag * jnp.sin(lam_im * dt)
    den = lam_re * lam_re + lam_im * lam_im
    nr = ar - 1.0
    qr = (nr * lam_re + ai * lam_im) / den
    qi = (ai * lam_re - nr * lam_im) / den
    n_cols = SSM_GROUPS * SSM_STATE
    a_re[...] = jnp.broadcast_to(ar, (SUBLANES, n_cols))
    a_im[...] = jnp.broadcast_to(ai, (SUBLANES, n_cols))

    tile_c = _lane_tiler(SSM_GROUP_CH, HALF_GROUPS)
    tile_n = _lane_tiler(SSM_STATE, HALF_GROUPS)
    mask_b = _group_mask((HALF_STATE, MXU_TILE), SSM_STATE, SSM_GROUP_CH)
    mask_c = _group_mask((MXU_TILE, HALF_STATE), SSM_GROUP_CH, SSM_STATE)
    for k, (b_out, c_out) in enumerate(((b_lo, c_lo), (b_hi, c_hi))):
        rows = slice(k * HALF_STATE, (k + 1) * HALF_STATE)
        q_re = jnp.broadcast_to(qr[:, rows], (128, HALF_STATE)).T[:, :SSM_GROUP_CH]
        q_im = jnp.broadcast_to(qi[:, rows], (128, HALF_STATE)).T[:, :SSM_GROUP_CH]
        b_re = b_re_ref[rows, :]
        b_im = b_im_ref[rows, :]
        bbar = (q_re * b_re - q_im * b_im, q_re * b_im + q_im * b_re)
        for j, bb in enumerate(bbar):
            wide = jnp.dot(bb.astype(BF16), tile_c, preferred_element_type=F32)
            blk = jnp.where(mask_b, wide, 0.0).T.astype(BF16)
            for p in range(LANE_BLOCKS):
                col = _state_col(0, p, j)
                b_out[:, col:col + LANES] = blk[:, p * LANES:(p + 1) * LANES]
        crow = slice(k * MXU_TILE, (k + 1) * MXU_TILE)
        for j, (c_ref, sign) in enumerate(((c_re_ref, 1.0), (c_im_ref, -1.0))):
            wide = jnp.dot(c_ref[crow, :].astype(BF16), tile_n, preferred_element_type=F32)
            blk = jnp.where(mask_c, sign * wide, 0.0).T.astype(BF16)
            for p in range(LANE_BLOCKS):
                row = _state_col(0, p, j)
                c_out[row:row + LANES, :] = blk[p * LANES:(p + 1) * LANES, :]

    mask_p = _group_mask((MXU_TILE, MXU_TILE), POOL_GROUP_CH, POOL_GROUP_CH)
    for k, wp in enumerate((wp_lo, wp_hi)):
        pw = pw_ref[k * MXU_TILE:(k + 1) * MXU_TILE, :]
        wp[...] = jnp.where(mask_p, jnp.concatenate([pw, pw], axis=1), 0.0).astype(BF16)


def _prepare_weights(w_in, lam_re, lam_im, log_dt, b_re, b_im, c_re, c_im, d, glu_w, glu_b, pool_w,
                     pool_scale, w_out, ln_g, ln_b):
    n_cols = SSM_GROUPS * SSM_STATE
    log_dt_cols = jnp.broadcast_to(log_dt.astype(F32)[:, None], (SSM_GROUPS, SSM_STATE))
    bf = lambda shape: jax.ShapeDtypeStruct(shape, BF16)
    out_shape = dict(w_s=bf((D_MODEL, SSM_WIDTH)), w_g=bf((D_MODEL, SSM_WIDTH + 2 * POOL_WIDTH)),
                     b_lo=bf((MXU_TILE, 2 * HALF_STATE)), b_hi=bf((MXU_TILE, 2 * HALF_STATE)),
                     a_re=jax.ShapeDtypeStruct((SUBLANES, n_cols), F32),
                     a_im=jax.ShapeDtypeStruct((SUBLANES, n_cols), F32),
                     c_lo=bf((2 * HALF_STATE, MXU_TILE)), c_hi=bf((2 * HALF_STATE, MXU_TILE)),
                     glu_w=bf((SSM_WIDTH, SSM_WIDTH)), wp_lo=bf((MXU_TILE, MXU_TILE)),
                     wp_hi=bf((MXU_TILE, MXU_TILE)), w_out=bf((D_MODEL, D_MODEL)))
    names = list(out_shape)
    outs = pl.pallas_call(
        _prep_kernel,
        out_shape=[out_shape[k] for k in names],
        compiler_params=pltpu.CompilerParams(vmem_limit_bytes=VMEM_LIMIT_BYTES),
        name="prepare_weights",
    )(lam_re.astype(F32).reshape(1, n_cols), lam_im.astype(F32).reshape(1, n_cols),
      log_dt_cols.reshape(1, n_cols), b_re.astype(F32).reshape(n_cols, SSM_GROUP_CH),
      b_im.astype(F32).reshape(n_cols, SSM_GROUP_CH),
      c_re.astype(F32).reshape(SSM_WIDTH, SSM_STATE), c_im.astype(F32).reshape(SSM_WIDTH, SSM_STATE),
      pool_w.astype(F32).reshape(POOL_WIDTH, POOL_GROUP_CH), w_in.astype(F32), glu_w.astype(F32),
      w_out.astype(F32))
    w = dict(zip(names, outs))
    row = lambda v: v.astype(F32).reshape(1, -1)
    w.update(d=row(d), glu_b=row(glu_b), pscale=row(pool_scale), ln_g=row(ln_g), ln_b=row(ln_b))
    return w


_WEIGHT_ORDER = ('w_s', 'w_g', 'b_lo', 'b_hi', 'a_re', 'a_im', 'c_lo', 'c_hi', 'd', 'glu_w', 'glu_b',
                 'wp_lo', 'wp_hi', 'pscale', 'w_out', 'ln_g', 'ln_b')


def _resident(shape):
    return pl.BlockSpec(shape, lambda s: (0,) * len(shape), pipeline_mode=pl.Buffered(1))


def _prompt_layer(x, w):
    bsz, seq, _ = x.shape
    assert bsz == SUBLANES and seq % CHUNK_TOKENS == 0 and seq >= 2 * CHUNK_TOKENS
    rows = CHUNK_TOKENS * SUBLANES
    halo = HALO_TOKENS * SUBLANES
    n_cols = SSM_GROUPS * SSM_STATE
    weights = [w[k] for k in _WEIGHT_ORDER]
    y, h_re, h_im, ptail = pl.pallas_call(
        _prompt_kernel,
        grid=(seq // CHUNK_TOKENS + 1,),
        in_specs=[pl.BlockSpec(memory_space=pl.ANY)] + [_resident(a.shape) for a in weights],
        out_specs=[pl.BlockSpec(memory_space=pl.ANY),
                   pl.BlockSpec((SUBLANES, n_cols), lambda s: (0, 0)),
                   pl.BlockSpec((SUBLANES, n_cols), lambda s: (0, 0)),
                   pl.BlockSpec((halo, POOL_WIDTH), lambda s: (0, 0))],
        out_shape=[jax.ShapeDtypeStruct(x.shape, F32),
                   jax.ShapeDtypeStruct((SUBLANES, n_cols), F32),
                   jax.ShapeDtypeStruct((SUBLANES, n_cols), F32),
                   jax.ShapeDtypeStruct((halo, POOL_WIDTH), F32)],
        scratch_shapes=[pltpu.VMEM((2, CHUNK_TOKENS, SUBLANES, D_MODEL), F32),
                        pltpu.VMEM((Y_SLOTS, CHUNK_TOKENS, SUBLANES, D_MODEL), F32),
                        pltpu.VMEM((rows, STATE_COLS), F32),
                        pltpu.VMEM((rows, STATE_COLS), BF16),
                        pltpu.VMEM((halo + rows, POOL_WIDTH), F32),
                        pltpu.SemaphoreType.DMA((2,)),
                        pltpu.SemaphoreType.DMA((Y_SLOTS,))],
        compiler_params=pltpu.CompilerParams(dimension_semantics=("arbitrary",),
                                             vmem_limit_bytes=VMEM_LIMIT_BYTES),
        name="prompt_layer",
    )(x, *weights)
    buf = ptail.reshape(HALO_TOKENS, SUBLANES, POOL_WIDTH)[HALO_TOKENS - POOL_BUF:]
    state_shape = (bsz, SSM_GROUPS, SSM_STATE)
    return y, h_re.reshape(state_shape), h_im.reshape(state_shape), jnp.swapaxes(buf, 0, 1)


def _sample_layer(x, h_re, h_im, pool_prefix, w):
    bsz = x.shape[0]
    n_cols = SSM_GROUPS * SSM_STATE
    weights = [w[k] for k in _WEIGHT_ORDER]
    prefix = jnp.swapaxes(pool_prefix.astype(F32), 0, 1)
    vmem = pl.BlockSpec(memory_space=pltpu.VMEM)
    hbm = pl.BlockSpec(memory_space=pl.ANY)
    y, n_re, n_im, p_in = pl.pallas_call(
        _sample_kernel,
        in_specs=[hbm] + [vmem] * (3 + len(weights)),
        out_specs=[hbm, vmem, vmem, vmem],
        out_shape=[jax.ShapeDtypeStruct((bsz, 1, D_MODEL), F32),
                   jax.ShapeDtypeStruct((bsz, n_cols), F32),
                   jax.ShapeDtypeStruct((bsz, n_cols), F32),
                   jax.ShapeDtypeStruct((bsz, POOL_WIDTH), F32)],
        scratch_shapes=[pltpu.VMEM((bsz, D_MODEL), F32),
                        pltpu.VMEM((bsz, D_MODEL), F32),
                        pltpu.VMEM((bsz, STATE_COLS), BF16),
                        pltpu.SemaphoreType.DMA((2,))],
        compiler_params=pltpu.CompilerParams(vmem_limit_bytes=VMEM_LIMIT_BYTES),
        name="sample_layer",
    )(x, h_re.astype(F32).reshape(bsz, n_cols), h_im.astype(F32).reshape(bsz, n_cols), prefix, *weights)
    state_shape = (bsz, SSM_GROUPS, SSM_STATE)
    buf = jnp.concatenate([pool_prefix.astype(F32)[:, 1:], p_in[:, None, :]], axis=1)
    return y, n_re.reshape(state_shape), n_im.reshape(state_shape), buf


def kernel(x_prompt, x_sample, state_ssm_re, state_ssm_im, state_pool, w_in, ssm_lambda_re, ssm_lambda_im, ssm_log_dt, ssm_b_re, ssm_b_im, ssm_c_re, ssm_c_im, ssm_d, glu_w, glu_b, pool_w, pool_scale, w_out, ln_g, ln_b):
    out_dtype = x_prompt.dtype
    depth = w_in.shape[0]
    hp = x_prompt.astype(F32)
    hs = x_sample.astype(F32)
    p_re, p_im, p_buf, s_re, s_im, s_buf = [], [], [], [], [], []
    for l in range(depth):
        w = _prepare_weights(w_in[l], ssm_lambda_re[l], ssm_lambda_im[l], ssm_log_dt[l], ssm_b_re[l],
                             ssm_b_im[l], ssm_c_re[l], ssm_c_im[l], ssm_d[l], glu_w[l], glu_b[l],
                             pool_w[l], pool_scale[l], w_out[l], ln_g[l], ln_b[l])
        hp, hr, hi, buf = _prompt_layer(hp, w)
        p_re.append(hr); p_im.append(hi); p_buf.append(buf)
        hs, hr, hi, buf = _sample_layer(hs, state_ssm_re[l], state_ssm_im[l], state_pool[l], w)
        s_re.append(hr); s_im.append(hi); s_buf.append(buf)
    return (hp.astype(out_dtype), hs.astype(out_dtype),
            jnp.stack(p_re), jnp.stack(p_im), jnp.stack(p_buf),
            jnp.stack(s_re), jnp.stack(s_im), jnp.stack(s_buf))
```

```python
import jax
import jax.numpy as jnp
from jax import lax
from jax.experimental import pallas as pl
from jax.experimental.pallas import tpu as pltpu

D_MODEL = 1024
SSM_WIDTH = 512
POOL_WIDTH = 512
SSM_GROUPS = 32
SSM_GROUP_CH = 16
SSM_STATE = 64
POOL_WINDOWS = (2, 4, 8, 16)
POOL_GROUP_CH = 128
POOL_BUF = 15
DN_ALPHA = 2.0 ** 0.25
LN_EPS = 1e-5

SUBLANES = 8
MXU_TILE = 256
HALF_GROUPS = MXU_TILE // SSM_GROUP_CH
HALF_STATE = HALF_GROUPS * SSM_STATE
LANES = 128
STATE_COLS = 4 * HALF_STATE
LANE_BLOCKS = HALF_STATE // LANES
CHUNK_TOKENS = 64
TAIL_BLOCKS = 2
X_SLOTS = 3
HALO_TOKENS = 16
VMEM_LIMIT_BYTES = 48 * 1024 * 1024

F32 = jnp.float32
BF16 = jnp.bfloat16


def _dot(a, b):
    return jnp.dot(a.astype(BF16), b, preferred_element_type=F32)


def _state_col(half, block, im):
    return half * 2 * HALF_STATE + block * 2 * LANES + im * LANES


def _ssm_input(xb, w_s_ref, b_lo_ref, b_hi_ref):
    s_in = jnp.dot(xb, w_s_ref[...], preferred_element_type=F32)
    sb = s_in.astype(BF16)
    bu_lo = jnp.dot(sb[:, :MXU_TILE], b_lo_ref[...], preferred_element_type=F32)
    bu_hi = jnp.dot(sb[:, MXU_TILE:], b_hi_ref[...], preferred_element_type=F32)
    return s_in, bu_lo, bu_hi


def _layer_norm(r, ln_g_ref, ln_b_ref):
    mu = jnp.mean(r, axis=-1, keepdims=True)
    rc = r - mu
    var = jnp.mean(rc * rc, axis=-1, keepdims=True)
    return rc * lax.rsqrt(var + LN_EPS) * ln_g_ref[...] + ln_b_ref[...]


def _mix(x, xb, s_in, hb, pooled, n_blocks, w_g_ref, c_lo_ref, c_hi_ref, d_ref, glu_w_ref,
         glu_b_ref, wp_lo_ref, wp_hi_ref, pscale_ref, w_out_ref):
    blk = x.shape[0] // n_blocks
    rs = [slice(i * blk, (i + 1) * blk) for i in range(n_blocks)]
    sy = []
    for r in rs:
        y_lo = jnp.dot(hb[r, :2 * HALF_STATE], c_lo_ref[...], preferred_element_type=F32)
        y_hi = jnp.dot(hb[r, 2 * HALF_STATE:], c_hi_ref[...], preferred_element_type=F32)
        sy.append(jnp.concatenate([y_lo, y_hi], axis=-1) + d_ref[...] * s_in[r])
    s_gate = jnp.dot(xb, w_g_ref[:, :SSM_WIDTH], preferred_element_type=F32)
    sy = [jax.nn.gelu(v) for v in sy]
    glu = [_dot(v, glu_w_ref[...]) + glu_b_ref[...] for v in sy]
    p_gate = jnp.dot(xb, w_g_ref[:, SSM_WIDTH + POOL_WIDTH:], preferred_element_type=F32)
    pb = pooled.astype(BF16)
    py = [jnp.concatenate([jnp.dot(pb[r, :MXU_TILE], wp_lo_ref[...], preferred_element_type=F32),
                           jnp.dot(pb[r, MXU_TILE:], wp_hi_ref[...], preferred_element_type=F32)], axis=-1)
          * pscale_ref[...] for r in rs]
    mixed = [jnp.concatenate([sy[i] * jax.nn.sigmoid(glu[i]) * jax.nn.silu(s_gate[r]),
                              py[i] * jax.nn.silu(p_gate[r])], axis=-1) for i, r in enumerate(rs)]
    return [DN_ALPHA * x[r] + _dot(mixed[i], w_out_ref[...]) for i, r in enumerate(rs)]


def _in_copies(x_hbm, xbuf, sem, chunk, slot):
    return [pltpu.make_async_copy(x_hbm.at[b, pl.ds(chunk * CHUNK_TOKENS, CHUNK_TOKENS), :],
                                  xbuf.at[slot, :, b, :], sem.at[slot]) for b in range(SUBLANES)]


def _out_copies(ybuf, y_hbm, sem, chunk, slot):
    return [pltpu.make_async_copy(ybuf.at[slot, :, b, :],
                                  y_hbm.at[b, pl.ds(chunk * CHUNK_TOKENS, CHUNK_TOKENS), :],
                                  sem.at[slot]) for b in range(SUBLANES)]


def _prompt_kernel(x_hbm, w_s_ref, w_g_ref, b_lo_ref, b_hi_ref, a_re_ref, a_im_ref, c_lo_ref, c_hi_ref,
                   d_ref, glu_w_ref, glu_b_ref, wp_lo_ref, wp_hi_ref, pscale_ref, w_out_ref, ln_g_ref,
                   ln_b_ref, y_hbm, hre_ref, him_ref, ptail_ref, xbuf, ybuf, hbuf, hb16, pbuf, sem_in, sem_out):
    s = pl.program_id(0)
    n = pl.num_programs(0)
    xslot = lax.rem(s, X_SLOTS)
    yslot = lax.rem(s, 2)
    ahead = jnp.minimum(s + X_SLOTS - 1, n - 1)
    ahead_slot = lax.rem(s + X_SLOTS - 1, X_SLOTS)
    rows = CHUNK_TOKENS * SUBLANES
    halo = HALO_TOKENS * SUBLANES

    @pl.when(s == 0)
    def _():
        for c0 in range(X_SLOTS - 1):
            for c in _in_copies(x_hbm, xbuf, sem_in, c0, c0):
                c.start()
        hre_ref[...] = jnp.zeros_like(hre_ref)
        him_ref[...] = jnp.zeros_like(him_ref)
        pbuf[0:halo, :] = jnp.zeros((halo, POOL_WIDTH), F32)

    for c in _in_copies(x_hbm, xbuf, sem_in, s, xslot):
        c.wait()

    x = xbuf[xslot].reshape(rows, D_MODEL)
    xb = x.astype(BF16)
    s_in, bu_lo, bu_hi = _ssm_input(xb, w_s_ref, b_lo_ref, b_hi_ref)
    hbuf[:, :2 * HALF_STATE] = bu_lo
    hbuf[:, 2 * HALF_STATE:] = bu_hi

    p_in = jnp.dot(xb, w_g_ref[:, SSM_WIDTH:SSM_WIDTH + POOL_WIDTH], preferred_element_type=F32)
    pbuf[halo:halo + rows, :] = p_in

    for k in range(2):
        for p in range(LANE_BLOCKS):
            c_re, c_im = _state_col(k, p, 0), _state_col(k, p, 1)
            a_col = k * HALF_STATE + p * LANES
            ar = a_re_ref[:, a_col:a_col + LANES]
            ai = a_im_ref[:, a_col:a_col + LANES]
            hr = hre_ref[:, a_col:a_col + LANES]
            hi = him_ref[:, a_col:a_col + LANES]
            for t2 in range(CHUNK_TOKENS // 2):
                pair = []
                for t in (2 * t2, 2 * t2 + 1):
                    r = t * SUBLANES
                    hr, hi = (ar * hr - ai * hi + hbuf[r:r + SUBLANES, c_re:c_re + LANES],
                              ar * hi + ai * hr + hbuf[r:r + SUBLANES, c_im:c_im + LANES])
                    pair.append((hr, hi))
                r2 = 2 * t2 * SUBLANES
                hb16[r2:r2 + 2 * SUBLANES, c_re:c_re + LANES] = jnp.concatenate(
                    [pair[0][0], pair[1][0]], axis=0).astype(BF16)
                hb16[r2:r2 + 2 * SUBLANES, c_im:c_im + LANES] = jnp.concatenate(
                    [pair[0][1], pair[1][1]], axis=0).astype(BF16)
            hre_ref[:, a_col:a_col + LANES] = hr
            him_ref[:, a_col:a_col + LANES] = hi

    row = lax.broadcasted_iota(jnp.int32, (rows, POOL_GROUP_CH), 0)
    pos1 = s * CHUNK_TOKENS + lax.shift_right_logical(row, 3) + 1
    pooled = []
    for g, w in enumerate(POOL_WINDOWS):
        acc = pbuf[:, g * POOL_GROUP_CH:(g + 1) * POOL_GROUP_CH]
        shift = SUBLANES
        while shift < w * SUBLANES:
            acc = acc[shift:, :] + acc[:-shift, :]
            shift *= 2
        acc = acc[acc.shape[0] - rows:, :]
        cnt = jnp.minimum(pos1, w).astype(F32)
        pooled.append(acc / cnt - p_in[:, g * POOL_GROUP_CH:(g + 1) * POOL_GROUP_CH])
    pooled = jnp.concatenate(pooled, axis=-1)
    pbuf[0:halo, :] = pbuf[rows:rows + halo, :]
    ptail_ref[...] = pbuf[0:halo, :]

    blk_tokens = CHUNK_TOKENS // TAIL_BLOCKS
    res = _mix(x, xb, s_in, hb16, pooled, TAIL_BLOCKS, w_g_ref, c_lo_ref, c_hi_ref, d_ref,
               glu_w_ref, glu_b_ref, wp_lo_ref, wp_hi_ref, pscale_ref, w_out_ref)
    for i, r in enumerate(res):
        y = _layer_norm(r, ln_g_ref, ln_b_ref)
        ybuf[yslot, i * blk_tokens:(i + 1) * blk_tokens] = y.reshape(blk_tokens, SUBLANES, D_MODEL)

    for c in _out_copies(ybuf, y_hbm, sem_out, s, yslot):
        c.start()
    for c in _in_copies(x_hbm, xbuf, sem_in, ahead, ahead_slot):
        c.start()

    @pl.when(s >= 1)
    def _():
        for c in _out_copies(ybuf, y_hbm, sem_out, s - 1, 1 - yslot):
            c.wait()

    @pl.when(s == n - 1)
    def _():
        for c in _out_copies(ybuf, y_hbm, sem_out, s, yslot):
            c.wait()
        for back in range(X_SLOTS - 1):
            for c in _in_copies(x_hbm, xbuf, sem_in, n - 1, lax.rem(s + X_SLOTS - 1 - back, X_SLOTS)):
                c.wait()


def _sample_kernel(x_hbm, h0_re_ref, h0_im_ref, prefix_ref, w_s_ref, w_g_ref, b_lo_ref, b_hi_ref, a_re_ref,
                   a_im_ref, c_lo_ref, c_hi_ref, d_ref, glu_w_ref, glu_b_ref, wp_lo_ref, wp_hi_ref, pscale_ref,
                   w_out_ref, ln_g_ref, ln_b_ref, y_hbm, hre_ref, him_ref, pin_ref, xs, ys, hb16, sem):
    load = pltpu.make_async_copy(x_hbm.at[:, 0, :], xs, sem.at[0])
    load.start()
    load.wait()
    x = xs[...]
    xb = x.astype(BF16)
    s_in, bu_lo, bu_hi = _ssm_input(xb, w_s_ref, b_lo_ref, b_hi_ref)
    bu = (bu_lo, bu_hi)
    for k in range(2):
        for p in range(LANE_BLOCKS):
            c_re, c_im = _state_col(k, p, 0), _state_col(k, p, 1)
            a_col = k * HALF_STATE + p * LANES
            ar = a_re_ref[0:1, a_col:a_col + LANES]
            ai = a_im_ref[0:1, a_col:a_col + LANES]
            hr = h0_re_ref[:, a_col:a_col + LANES]
            hi = h0_im_ref[:, a_col:a_col + LANES]
            b_re, b_im = _state_col(0, p, 0), _state_col(0, p, 1)
            nhr = ar * hr - ai * hi + bu[k][:, b_re:b_re + LANES]
            nhi = ar * hi + ai * hr + bu[k][:, b_im:b_im + LANES]
            hre_ref[:, a_col:a_col + LANES] = nhr
            him_ref[:, a_col:a_col + LANES] = nhi
            hb16[:, c_re:c_re + LANES] = nhr.astype(BF16)
            hb16[:, c_im:c_im + LANES] = nhi.astype(BF16)

    p_in = jnp.dot(xb, w_g_ref[:, SSM_WIDTH:SSM_WIDTH + POOL_WIDTH], preferred_element_type=F32)
    pin_ref[...] = p_in
    pooled = []
    for g, w in enumerate(POOL_WINDOWS):
        cols = slice(g * POOL_GROUP_CH, (g + 1) * POOL_GROUP_CH)
        acc = p_in[:, cols]
        for j in range(1, w):
            acc = acc + prefix_ref[POOL_BUF - j, :, cols]
        pooled.append(acc / float(w) - p_in[:, cols])
    pooled = jnp.concatenate(pooled, axis=-1)
    res = _mix(x, xb, s_in, hb16, pooled, 1, w_g_ref, c_lo_ref, c_hi_ref, d_ref, glu_w_ref, glu_b_ref,
               wp_lo_ref, wp_hi_ref, pscale_ref, w_out_ref)[0]
    ys[...] = _layer_norm(res, ln_g_ref, ln_b_ref)
    store = pltpu.make_async_copy(ys, y_hbm.at[:, 0, :], sem.at[1])
    store.start()
    store.wait()


def _group_mask(shape, row_group, col_group):
    r = lax.broadcasted_iota(jnp.int32, shape, 0) // row_group
    c = lax.broadcasted_iota(jnp.int32, shape, 1) // col_group
    return r == c


def _lane_tiler(width, copies):
    shape = (width, width * copies)
    r = lax.broadcasted_iota(jnp.int32, shape, 0)
    c = lax.broadcasted_iota(jnp.int32, shape, 1) % width
    return (r == c).astype(BF16)


def _prep_kernel(lam_re_ref, lam_im_ref, log_dt_ref, b_re_ref, b_im_ref, c_re_ref, c_im_ref, pw_ref, w_in_ref,
                 glu_w_ref, w_out_ref,
                 w_s16, w_g16, b_lo, b_hi, a_re, a_im, c_lo, c_hi, glu16, wp_lo, wp_hi, w_out16):
    w_s16[...] = w_in_ref[:, :SSM_WIDTH].astype(BF16)
    w_g16[...] = w_in_ref[:, SSM_WIDTH:].astype(BF16)
    glu16[...] = glu_w_ref[...].astype(BF16)
    w_out16[...] = w_out_ref[...].astype(BF16)

    lam_re = lam_re_ref[...]
    lam_im = lam_im_ref[...]
    dt = jnp.exp(log_dt_ref[...])
    mag = jnp.exp(lam_re * dt)
    ar, ai = mag * jnp.cos(lam_im * dt), mag * jnp.sin(lam_im * dt)
    den = lam_re * lam_re + lam_im * lam_im
    nr = ar - 1.0
    qr = (nr * lam_re + ai * lam_im) / den
    qi = (ai * lam_re - nr * lam_im) / den
    n_cols = SSM_GROUPS * SSM_STATE
    a_re[...] = jnp.broadcast_to(ar, (SUBLANES, n_cols))
    a_im[...] = jnp.broadcast_to(ai, (SUBLANES, n_cols))

    tile_c = _lane_tiler(SSM_GROUP_CH, HALF_GROUPS)
    tile_n = _lane_tiler(SSM_STATE, HALF_GROUPS)
    mask_b = _group_mask((HALF_STATE, MXU_TILE), SSM_STATE, SSM_GROUP_CH)
    mask_c = _group_mask((MXU_TILE, HALF_STATE), SSM_GROUP_CH, SSM_STATE)
    for k, (b_out, c_out) in enumerate(((b_lo, c_lo), (b_hi, c_hi))):
        rows = slice(k * HALF_STATE, (k + 1) * HALF_STATE)
        q_re = jnp.broadcast_to(qr[:, rows], (128, HALF_STATE)).T[:, :SSM_GROUP_CH]
        q_im = jnp.broadcast_to(qi[:, rows], (128, HALF_STATE)).T[:, :SSM_GROUP_CH]
        b_re = b_re_ref[rows, :]
        b_im = b_im_ref[rows, :]
        bbar = (q_re * b_re - q_im * b_im, q_re * b_im + q_im * b_re)
        for j, bb in enumerate(bbar):
            wide = jnp.dot(bb.astype(BF16), tile_c, preferred_element_type=F32)
            blk = jnp.where(mask_b, wide, 0.0).T.astype(BF16)
            for p in range(LANE_BLOCKS):
                col = _state_col(0, p, j)
                b_out[:, col:col + LANES] = blk[:, p * LANES:(p + 1) * LANES]
        crow = slice(k * MXU_TILE, (k + 1) * MXU_TILE)
        for j, (c_ref, sign) in enumerate(((c_re_ref, 1.0), (c_im_ref, -1.0))):
            wide = jnp.dot(c_ref[crow, :].astype(BF16), tile_n, preferred_element_type=F32)
            blk = jnp.where(mask_c, sign * wide, 0.0).T.astype(BF16)
            for p in range(LANE_BLOCKS):
                row = _state_col(0, p, j)
                c_out[row:row + LANES, :] = blk[p * LANES:(p + 1) * LANES, :]

    mask_p = _group_mask((MXU_TILE, MXU_TILE), POOL_GROUP_CH, POOL_GROUP_CH)
    for k, wp in enumerate((wp_lo, wp_hi)):
        pw = pw_ref[k * MXU_TILE:(k + 1) * MXU_TILE, :]
        wp[...] = jnp.where(mask_p, jnp.concatenate([pw, pw], axis=1), 0.0).astype(BF16)


def _prepare_weights(w_in, lam_re, lam_im, log_dt, b_re, b_im, c_re, c_im, d, glu_w, glu_b, pool_w,
                     pool_scale, w_out, ln_g, ln_b):
    n_cols = SSM_GROUPS * SSM_STATE
    log_dt_cols = jnp.broadcast_to(log_dt.astype(F32)[:, None], (SSM_GROUPS, SSM_STATE))
    bf = lambda shape: jax.ShapeDtypeStruct(shape, BF16)
    out_shape = dict(w_s=bf((D_MODEL, SSM_WIDTH)), w_g=bf((D_MODEL, SSM_WIDTH + 2 * POOL_WIDTH)),
                     b_lo=bf((MXU_TILE, 2 * HALF_STATE)), b_hi=bf((MXU_TILE, 2 * HALF_STATE)),
                     a_re=jax.ShapeDtypeStruct((SUBLANES, n_cols), F32),
                     a_im=jax.ShapeDtypeStruct((SUBLANES, n_cols), F32),
                     c_lo=bf((2 * HALF_STATE, MXU_TILE)), c_hi=bf((2 * HALF_STATE, MXU_TILE)),
                     glu_w=bf((SSM_WIDTH, SSM_WIDTH)), wp_lo=bf((MXU_TILE, MXU_TILE)),
                     wp_hi=bf((MXU_TILE, MXU_TILE)), w_out=bf((D_MODEL, D_MODEL)))
    names = list(out_shape)
    outs = pl.pallas_call(
        _prep_kernel,
        out_shape=[out_shape[k] for k in names],
        compiler_params=pltpu.CompilerParams(vmem_limit_bytes=VMEM_LIMIT_BYTES),
        name="prepare_weights",
    )(lam_re.astype(F32).reshape(1, n_cols), lam_im.astype(F32).reshape(1, n_cols),
      log_dt_cols.reshape(1, n_cols), b_re.astype(F32).reshape(n_cols, SSM_GROUP_CH),
      b_im.astype(F32).reshape(n_cols, SSM_GROUP_CH),
      c_re.astype(F32).reshape(SSM_WIDTH, SSM_STATE), c_im.astype(F32).reshape(SSM_WIDTH, SSM_STATE),
      pool_w.astype(F32).reshape(POOL_WIDTH, POOL_GROUP_CH), w_in.astype(F32), glu_w.astype(F32),
      w_out.astype(F32))
    w = dict(zip(names, outs))
    row = lambda v: v.astype(F32).reshape(1, -1)
    w.update(d=row(d), glu_b=row(glu_b), pscale=row(pool_scale), ln_g=row(ln_g), ln_b=row(ln_b))
    return w


_WEIGHT_ORDER = ('w_s', 'w_g', 'b_lo', 'b_hi', 'a_re', 'a_im', 'c_lo', 'c_hi', 'd', 'glu_w', 'glu_b',
                 'wp_lo', 'wp_hi', 'pscale', 'w_out', 'ln_g', 'ln_b')


def _resident(shape):
    return pl.BlockSpec(shape, lambda s: (0,) * len(shape), pipeline_mode=pl.Buffered(1))


def _prompt_layer(x, w):
    bsz, seq, _ = x.shape
    assert bsz == SUBLANES and seq % CHUNK_TOKENS == 0 and seq >= X_SLOTS * CHUNK_TOKENS
    rows = CHUNK_TOKENS * SUBLANES
    halo = HALO_TOKENS * SUBLANES
    n_cols = SSM_GROUPS * SSM_STATE
    weights = [w[k] for k in _WEIGHT_ORDER]
    y, h_re, h_im, ptail = pl.pallas_call(
        _prompt_kernel,
        grid=(seq // CHUNK_TOKENS,),
        in_specs=[pl.BlockSpec(memory_space=pl.ANY)] + [_resident(a.shape) for a in weights],
        out_specs=[pl.BlockSpec(memory_space=pl.ANY),
                   pl.BlockSpec((SUBLANES, n_cols), lambda s: (0, 0)),
                   pl.BlockSpec((SUBLANES, n_cols), lambda s: (0, 0)),
                   pl.BlockSpec((halo, POOL_WIDTH), lambda s: (0, 0))],
        out_shape=[jax.ShapeDtypeStruct(x.shape, F32),
                   jax.ShapeDtypeStruct((SUBLANES, n_cols), F32),
                   jax.ShapeDtypeStruct((SUBLANES, n_cols), F32),
                   jax.ShapeDtypeStruct((halo, POOL_WIDTH), F32)],
        scratch_shapes=[pltpu.VMEM((X_SLOTS, CHUNK_TOKENS, SUBLANES, D_MODEL), F32),
                        pltpu.VMEM((2, CHUNK_TOKENS, SUBLANES, D_MODEL), F32),
                        pltpu.VMEM((rows, STATE_COLS), F32),
                        pltpu.VMEM((rows, STATE_COLS), BF16),
                        pltpu.VMEM((halo + rows, POOL_WIDTH), F32),
                        pltpu.SemaphoreType.DMA((X_SLOTS,)),
                        pltpu.SemaphoreType.DMA((2,))],
        compiler_params=pltpu.CompilerParams(dimension_semantics=("arbitrary",),
                                             vmem_limit_bytes=VMEM_LIMIT_BYTES),
        name="prompt_layer",
    )(x, *weights)
    buf = ptail.reshape(HALO_TOKENS, SUBLANES, POOL_WIDTH)[HALO_TOKENS - POOL_BUF:]
    state_shape = (bsz, SSM_GROUPS, SSM_STATE)
    return y, h_re.reshape(state_shape), h_im.reshape(state_shape), jnp.swapaxes(buf, 0, 1)


def _sample_layer(x, h_re, h_im, pool_prefix, w):
    bsz = x.shape[0]
    n_cols = SSM_GROUPS * SSM_STATE
    weights = [w[k] for k in _WEIGHT_ORDER]
    prefix = jnp.swapaxes(pool_prefix.astype(F32), 0, 1)
    vmem = pl.BlockSpec(memory_space=pltpu.VMEM)
    hbm = pl.BlockSpec(memory_space=pl.ANY)
    y, n_re, n_im, p_in = pl.pallas_call(
        _sample_kernel,
        in_specs=[hbm] + [vmem] * (3 + len(weights)),
        out_specs=[hbm, vmem, vmem, vmem],
        out_shape=[jax.ShapeDtypeStruct((bsz, 1, D_MODEL), F32),
                   jax.ShapeDtypeStruct((bsz, n_cols), F32),
                   jax.ShapeDtypeStruct((bsz, n_cols), F32),
                   jax.ShapeDtypeStruct((bsz, POOL_WIDTH), F32)],
        scratch_shapes=[pltpu.VMEM((bsz, D_MODEL), F32),
                        pltpu.VMEM((bsz, D_MODEL), F32),
                        pltpu.VMEM((bsz, STATE_COLS), BF16),
                        pltpu.SemaphoreType.DMA((2,))],
        compiler_params=pltpu.CompilerParams(vmem_limit_bytes=VMEM_LIMIT_BYTES),
        name="sample_layer",
    )(x, h_re.astype(F32).reshape(bsz, n_cols), h_im.astype(F32).reshape(bsz, n_cols), prefix, *weights)
    state_shape = (bsz, SSM_GROUPS, SSM_STATE)
    buf = jnp.concatenate([pool_prefix.astype(F32)[:, 1:], p_in[:, None, :]], axis=1)
    return y, n_re.reshape(state_shape), n_im.reshape(state_shape), buf


def kernel(x_prompt, x_sample, state_ssm_re, state_ssm_im, state_pool, w_in, ssm_lambda_re, ssm_lambda_im, ssm_log_dt, ssm_b_re, ssm_b_im, ssm_c_re, ssm_c_im, ssm_d, glu_w, glu_b, pool_w, pool_scale, w_out, ln_g, ln_b):
    out_dtype = x_prompt.dtype
    depth = w_in.shape[0]
    hp = x_prompt.astype(F32)
    hs = x_sample.astype(F32)
    p_re, p_im, p_buf, s_re, s_im, s_buf = [], [], [], [], [], []
    for l in range(depth):
        w = _prepare_weights(w_in[l], ssm_lambda_re[l], ssm_lambda_im[l], ssm_log_dt[l], ssm_b_re[l],
                             ssm_b_im[l], ssm_c_re[l], ssm_c_im[l], ssm_d[l], glu_w[l], glu_b[l],
                             pool_w[l], pool_scale[l], w_out[l], ln_g[l], ln_b[l])
        hp, hr, hi, buf = _prompt_layer(hp, w)
        p_re.append(hr); p_im.append(hi); p_buf.append(buf)
        hs, hr, hi, buf = _sample_layer(hs, state_ssm_re[l], state_ssm_im[l], state_pool[l], w)
        s_re.append(hr); s_im.append(hi); s_buf.append(buf)
    return (hp.astype(out_dtype), hs.astype(out_dtype),
            jnp.stack(p_re), jnp.stack(p_im), jnp.stack(p_buf),
            jnp.stack(s_re), jnp.stack(s_im), jnp.stack(s_buf))
```

```python
import jax
import jax.numpy as jnp
from jax import lax
from jax.experimental import pallas as pl
from jax.experimental.pallas import tpu as pltpu

D_MODEL = 1024
SSM_WIDTH = 512
POOL_WIDTH = 512
SSM_GROUPS = 32
SSM_GROUP_CH = 16
SSM_STATE = 64
POOL_WINDOWS = (2, 4, 8, 16)
POOL_GROUP_CH = 128
POOL_BUF = 15
DN_ALPHA = 2.0 ** 0.25
LN_EPS = 1e-5

SUBLANES = 8
MXU_TILE = 256
HALF_GROUPS = MXU_TILE // SSM_GROUP_CH
HALF_STATE = HALF_GROUPS * SSM_STATE
LANES = 128
STATE_COLS = 4 * HALF_STATE
LANE_BLOCKS = HALF_STATE // LANES
CHUNK_TOKENS = 64
TAIL_BLOCKS = 2
X_SLOTS = 3
HALO_TOKENS = 16
VMEM_LIMIT_BYTES = 48 * 1024 * 1024

F32 = jnp.float32
BF16 = jnp.bfloat16


def _dot(a, b):
    return jnp.dot(a.astype(BF16), b, preferred_element_type=F32)


def _state_col(half, block, im):
    return half * 2 * HALF_STATE + block * 2 * LANES + im * LANES


def _ssm_input(xb, w_s_ref, b_lo_ref, b_hi_ref):
    s_in = jnp.dot(xb, w_s_ref[...], preferred_element_type=F32)
    sb = s_in.astype(BF16)
    bu_lo = jnp.dot(sb[:, :MXU_TILE], b_lo_ref[...], preferred_element_type=F32)
    bu_hi = jnp.dot(sb[:, MXU_TILE:], b_hi_ref[...], preferred_element_type=F32)
    return s_in, bu_lo, bu_hi


def _layer_norm(r, ln_g_ref, ln_b_ref):
    mu = jnp.mean(r, axis=-1, keepdims=True)
    rc = r - mu
    var = jnp.mean(rc * rc, axis=-1, keepdims=True)
    return rc * lax.rsqrt(var + LN_EPS) * ln_g_ref[...] + ln_b_ref[...]


def _mix(x, xb, s_in, hb, pooled, n_blocks, w_g_ref, c_lo_ref, c_hi_ref, d_ref, glu_w_ref,
         glu_b_ref, wp_lo_ref, wp_hi_ref, pscale_ref, w_out_ref):
    blk = x.shape[0] // n_blocks
    rs = [slice(i * blk, (i + 1) * blk) for i in range(n_blocks)]
    sy = []
    for r in rs:
        y_lo = jnp.dot(hb[r, :2 * HALF_STATE], c_lo_ref[...], preferred_element_type=F32)
        y_hi = jnp.dot(hb[r, 2 * HALF_STATE:], c_hi_ref[...], preferred_element_type=F32)
        sy.append(jnp.concatenate([y_lo, y_hi], axis=-1) + d_ref[...] * s_in[r])
    s_gate = jnp.dot(xb, w_g_ref[:, :SSM_WIDTH], preferred_element_type=F32)
    sy = [jax.nn.gelu(v) for v in sy]
    glu = [_dot(v, glu_w_ref[...]) + glu_b_ref[...] for v in sy]
    p_gate = jnp.dot(xb, w_g_ref[:, SSM_WIDTH + POOL_WIDTH:], preferred_element_type=F32)
    pb = pooled.astype(BF16)
    py = [jnp.concatenate([jnp.dot(pb[r, :MXU_TILE], wp_lo_ref[...], preferred_element_type=F32),
                           jnp.dot(pb[r, MXU_TILE:], wp_hi_ref[...], preferred_element_type=F32)], axis=-1)
          * pscale_ref[...] for r in rs]
    mixed = [jnp.concatenate([sy[i] * jax.nn.sigmoid(glu[i]) * jax.nn.silu(s_gate[r]),
                              py[i] * jax.nn.silu(p_gate[r])], axis=-1) for i, r in enumerate(rs)]
    return [DN_ALPHA * x[r] + _dot(mixed[i], w_out_ref[...]) for i, r in enumerate(rs)]


def _in_copies(x_hbm, xbuf, sem, chunk, slot):
    return [pltpu.make_async_copy(x_hbm.at[b, pl.ds(chunk * CHUNK_TOKENS, CHUNK_TOKENS), :],
                                  xbuf.at[slot, :, b, :], sem.at[slot]) for b in range(SUBLANES)]


def _out_copies(ybuf, y_hbm, sem, chunk, slot):
    return [pltpu.make_async_copy(ybuf.at[slot, :, b, :],
                                  y_hbm.at[b, pl.ds(chunk * CHUNK_TOKENS, CHUNK_TOKENS), :],
                                  sem.at[slot]) for b in range(SUBLANES)]


def _prompt_kernel(x_hbm, w_s_ref, w_g_ref, b_lo_ref, b_hi_ref, a_re_ref, a_im_ref, c_lo_ref, c_hi_ref,
                   d_ref, glu_w_ref, glu_b_ref, wp_lo_ref, wp_hi_ref, pscale_ref, w_out_ref, ln_g_ref,
                   ln_b_ref, y_hbm, hre_ref, him_ref, ptail_ref, xbuf, ybuf, hbuf, hb16, pbuf, sem_in, sem_out):
    s = pl.program_id(0)
    n = pl.num_programs(0)
    xslot = lax.rem(s, X_SLOTS)
    yslot = lax.rem(s, 2)
    ahead = jnp.minimum(s + X_SLOTS - 1, n - 1)
    ahead_slot = lax.rem(s + X_SLOTS - 1, X_SLOTS)
    rows = CHUNK_TOKENS * SUBLANES
    halo = HALO_TOKENS * SUBLANES

    @pl.when(s == 0)
    def _():
        for c0 in range(X_SLOTS - 1):
            for c in _in_copies(x_hbm, xbuf, sem_in, c0, c0):
                c.start()
        hre_ref[...] = jnp.zeros_like(hre_ref)
        him_ref[...] = jnp.zeros_like(him_ref)
        pbuf[0:halo, :] = jnp.zeros((halo, POOL_WIDTH), F32)

    for c in _in_copies(x_hbm, xbuf, sem_in, s, xslot):
        c.wait()

    x = xbuf[xslot].reshape(rows, D_MODEL)
    xb = x.astype(BF16)
    s_in, bu_lo, bu_hi = _ssm_input(xb, w_s_ref, b_lo_ref, b_hi_ref)
    hbuf[:, :2 * HALF_STATE] = bu_lo
    hbuf[:, 2 * HALF_STATE:] = bu_hi

    p_in = jnp.dot(xb, w_g_ref[:, SSM_WIDTH:SSM_WIDTH + POOL_WIDTH], preferred_element_type=F32)
    pbuf[halo:halo + rows, :] = p_in

    for k in range(2):
        for p in range(LANE_BLOCKS):
            c_re, c_im = _state_col(k, p, 0), _state_col(k, p, 1)
            a_col = k * HALF_STATE + p * LANES
            ar = a_re_ref[:, a_col:a_col + LANES]
            ai = a_im_ref[:, a_col:a_col + LANES]
            hr = hre_ref[:, a_col:a_col + LANES]
            hi = him_ref[:, a_col:a_col + LANES]
            for t2 in range(CHUNK_TOKENS // 2):
                pair = []
                for t in (2 * t2, 2 * t2 + 1):
                    r = t * SUBLANES
                    hr, hi = (ar * hr - ai * hi + hbuf[r:r + SUBLANES, c_re:c_re + LANES],
                              ar * hi + ai * hr + hbuf[r:r + SUBLANES, c_im:c_im + LANES])
                    pair.append((hr, hi))
                r2 = 2 * t2 * SUBLANES
                hb16[r2:r2 + 2 * SUBLANES, c_re:c_re + LANES] = jnp.concatenate(
                    [pair[0][0], pair[1][0]], axis=0).astype(BF16)
                hb16[r2:r2 + 2 * SUBLANES, c_im:c_im + LANES] = jnp.concatenate(
                    [pair[0][1], pair[1][1]], axis=0).astype(BF16)
            hre_ref[:, a_col:a_col + LANES] = hr
            him_ref[:, a_col:a_col + LANES] = hi

    row = lax.broadcasted_iota(jnp.int32, (rows, POOL_GROUP_CH), 0)
    pos1 = s * CHUNK_TOKENS + lax.shift_right_logical(row, 3) + 1
    pooled = []
    for g, w in enumerate(POOL_WINDOWS):
        acc = pbuf[:, g * POOL_GROUP_CH:(g + 1) * POOL_GROUP_CH]
        shift = SUBLANES
        while shift < w * SUBLANES:
            acc = acc[shift:, :] + acc[:-shift, :]
            shift *= 2
        acc = acc[acc.shape[0] - rows:, :]
        cnt = jnp.minimum(pos1, w).astype(F32)
        pooled.append(acc / cnt - p_in[:, g * POOL_GROUP_CH:(g + 1) * POOL_GROUP_CH])
    pooled = jnp.concatenate(pooled, axis=-1)
    pbuf[0:halo, :] = pbuf[rows:rows + halo, :]
    ptail_ref[...] = pbuf[0:halo, :]

    blk_tokens = CHUNK_TOKENS // TAIL_BLOCKS
    res = _mix(x, xb, s_in, hb16, pooled, TAIL_BLOCKS, w_g_ref, c_lo_ref, c_hi_ref, d_ref,
               glu_w_ref, glu_b_ref, wp_lo_ref, wp_hi_ref, pscale_ref, w_out_ref)
    for i, r in enumerate(res):
        y = _layer_norm(r, ln_g_ref, ln_b_ref)
        ybuf[yslot, i * blk_tokens:(i + 1) * blk_tokens] = y.reshape(blk_tokens, SUBLANES, D_MODEL)

    for c in _out_copies(ybuf, y_hbm, sem_out, s, yslot):
        c.start()
    for c in _in_copies(x_hbm, xbuf, sem_in, ahead, ahead_slot):
        c.start()

    @pl.when(s >= 1)
    def _():
        for c in _out_copies(ybuf, y_hbm, sem_out, s - 1, 1 - yslot):
            c.wait()

    @pl.when(s == n - 1)
    def _():
        for c in _out_copies(ybuf, y_hbm, sem_out, s, yslot):
            c.wait()
        for back in range(X_SLOTS - 1):
            for c in _in_copies(x_hbm, xbuf, sem_in, n - 1, lax.rem(s + X_SLOTS - 1 - back, X_SLOTS)):
                c.wait()


def _sample_body(xs, h0_re_ref, h0_im_ref, prefix_ref, w_s_ref, w_g_ref, b_lo_ref, b_hi_ref, a_re_ref,
                 a_im_ref, c_lo_ref, c_hi_ref, d_ref, glu_w_ref, glu_b_ref, wp_lo_ref, wp_hi_ref, pscale_ref,
                 w_out_ref, ln_g_ref, ln_b_ref, ys, hre_ref, him_ref, pin_ref, hb16):
    x = xs[...]
    xb = x.astype(BF16)
    s_in, bu_lo, bu_hi = _ssm_input(xb, w_s_ref, b_lo_ref, b_hi_ref)
    bu = (bu_lo, bu_hi)
    for k in range(2):
        for p in range(LANE_BLOCKS):
            c_re, c_im = _state_col(k, p, 0), _state_col(k, p, 1)
            a_col = k * HALF_STATE + p * LANES
            ar = a_re_ref[0:1, a_col:a_col + LANES]
            ai = a_im_ref[0:1, a_col:a_col + LANES]
            hr = h0_re_ref[:, a_col:a_col + LANES]
            hi = h0_im_ref[:, a_col:a_col + LANES]
            b_re, b_im = _state_col(0, p, 0), _state_col(0, p, 1)
            nhr = ar * hr - ai * hi + bu[k][:, b_re:b_re + LANES]
            nhi = ar * hi + ai * hr + bu[k][:, b_im:b_im + LANES]
            hre_ref[:, a_col:a_col + LANES] = nhr
            him_ref[:, a_col:a_col + LANES] = nhi
            hb16[:, c_re:c_re + LANES] = nhr.astype(BF16)
            hb16[:, c_im:c_im + LANES] = nhi.astype(BF16)

    p_in = jnp.dot(xb, w_g_ref[:, SSM_WIDTH:SSM_WIDTH + POOL_WIDTH], preferred_element_type=F32)
    pin_ref[...] = p_in
    pooled = []
    for g, w in enumerate(POOL_WINDOWS):
        cols = slice(g * POOL_GROUP_CH, (g + 1) * POOL_GROUP_CH)
        acc = p_in[:, cols]
        for j in range(1, w):
            acc = acc + prefix_ref[POOL_BUF - j, :, cols]
        pooled.append(acc / float(w) - p_in[:, cols])
    pooled = jnp.concatenate(pooled, axis=-1)
    res = _mix(x, xb, s_in, hb16, pooled, 1, w_g_ref, c_lo_ref, c_hi_ref, d_ref, glu_w_ref, glu_b_ref,
               wp_lo_ref, wp_hi_ref, pscale_ref, w_out_ref)[0]
    ys[...] = _layer_norm(res, ln_g_ref, ln_b_ref)


def _group_mask(shape, row_group, col_group):
    r = lax.broadcasted_iota(jnp.int32, shape, 0) // row_group
    c = lax.broadcasted_iota(jnp.int32, shape, 1) // col_group
    return r == c


def _lane_tiler(width, copies):
    shape = (width, width * copies)
    r = lax.broadcasted_iota(jnp.int32, shape, 0)
    c = lax.broadcasted_iota(jnp.int32, shape, 1) % width
    return (r == c).astype(BF16)


def _prep_body(lam_re_ref, lam_im_ref, log_dt_ref, b_re_ref, b_im_ref, c_re_ref, c_im_ref, pw_ref, w_in_ref,
               glu_w_ref, w_out_ref,
               w_s16, w_g16, b_lo, b_hi, a_re, a_im, c_lo, c_hi, glu16, wp_lo, wp_hi, w_out16):
    w_s16[...] = w_in_ref[:, :SSM_WIDTH].astype(BF16)
    w_g16[...] = w_in_ref[:, SSM_WIDTH:].astype(BF16)
    glu16[...] = glu_w_ref[...].astype(BF16)
    w_out16[...] = w_out_ref[...].astype(BF16)

    lam_re = lam_re_ref[...]
    lam_im = lam_im_ref[...]
    dt = jnp.exp(log_dt_ref[...])
    mag = jnp.exp(lam_re * dt)
    ar, ai = mag * jnp.cos(lam_im * dt), mag * jnp.sin(lam_im * dt)
    den = lam_re * lam_re + lam_im * lam_im
    nr = ar - 1.0
    qr = (nr * lam_re + ai * lam_im) / den
    qi = (ai * lam_re - nr * lam_im) / den
    n_cols = SSM_GROUPS * SSM_STATE
    a_re[...] = jnp.broadcast_to(ar, (SUBLANES, n_cols))
    a_im[...] = jnp.broadcast_to(ai, (SUBLANES, n_cols))

    tile_c = _lane_tiler(SSM_GROUP_CH, HALF_GROUPS)
    tile_n = _lane_tiler(SSM_STATE, HALF_GROUPS)
    mask_b = _group_mask((HALF_STATE, MXU_TILE), SSM_STATE, SSM_GROUP_CH)
    mask_c = _group_mask((MXU_TILE, HALF_STATE), SSM_GROUP_CH, SSM_STATE)
    for k, (b_out, c_out) in enumerate(((b_lo, c_lo), (b_hi, c_hi))):
        rows = slice(k * HALF_STATE, (k + 1) * HALF_STATE)
        q_re = jnp.broadcast_to(qr[:, rows], (128, HALF_STATE)).T[:, :SSM_GROUP_CH]
        q_im = jnp.broadcast_to(qi[:, rows], (128, HALF_STATE)).T[:, :SSM_GROUP_CH]
        b_re = b_re_ref[rows, :]
        b_im = b_im_ref[rows, :]
        bbar = (q_re * b_re - q_im * b_im, q_re * b_im + q_im * b_re)
        for j, bb in enumerate(bbar):
            wide = jnp.dot(bb.astype(BF16), tile_c, preferred_element_type=F32)
            blk = jnp.where(mask_b, wide, 0.0).T.astype(BF16)
            for p in range(LANE_BLOCKS):
                col = _state_col(0, p, j)
                b_out[:, col:col + LANES] = blk[:, p * LANES:(p + 1) * LANES]
        crow = slice(k * MXU_TILE, (k + 1) * MXU_TILE)
        for j, (c_ref, sign) in enumerate(((c_re_ref, 1.0), (c_im_ref, -1.0))):
            wide = jnp.dot(c_ref[crow, :].astype(BF16), tile_n, preferred_element_type=F32)
            blk = jnp.where(mask_c, sign * wide, 0.0).T.astype(BF16)
            for p in range(LANE_BLOCKS):
                row = _state_col(0, p, j)
                c_out[row:row + LANES, :] = blk[p * LANES:(p + 1) * LANES, :]

    mask_p = _group_mask((MXU_TILE, MXU_TILE), POOL_GROUP_CH, POOL_GROUP_CH)
    for k, wp in enumerate((wp_lo, wp_hi)):
        pw = pw_ref[k * MXU_TILE:(k + 1) * MXU_TILE, :]
        wp[...] = jnp.where(mask_p, jnp.concatenate([pw, pw], axis=1), 0.0).astype(BF16)


N_RAW = 11
N_PREPARED = 12


def _prep_sample_kernel(*refs):
    raw = refs[:N_RAW]
    d_ref, glu_b_ref, pscale_ref, ln_g_ref, ln_b_ref, x_hbm, h0_re_ref, h0_im_ref, prefix_ref = refs[N_RAW:N_RAW + 9]
    prepared = refs[N_RAW + 9:N_RAW + 9 + N_PREPARED]
    y_hbm, hre_ref, him_ref, pin_ref, xs, ys, hb16, sem = refs[N_RAW + 9 + N_PREPARED:]
    w_s16, w_g16, b_lo, b_hi, a_re, a_im, c_lo, c_hi, glu16, wp_lo, wp_hi, w_out16 = prepared

    load = pltpu.make_async_copy(x_hbm.at[:, 0, :], xs, sem.at[0])
    load.start()
    _prep_body(*raw, *prepared)
    load.wait()
    _sample_body(xs, h0_re_ref, h0_im_ref, prefix_ref, w_s16, w_g16, b_lo, b_hi, a_re, a_im, c_lo, c_hi, d_ref,
                 glu16, glu_b_ref, wp_lo, wp_hi, pscale_ref, w_out16, ln_g_ref, ln_b_ref, ys, hre_ref, him_ref,
                 pin_ref, hb16)
    store = pltpu.make_async_copy(ys, y_hbm.at[:, 0, :], sem.at[1])
    store.start()
    store.wait()


_PREPARED_ORDER = ('w_s', 'w_g', 'b_lo', 'b_hi', 'a_re', 'a_im', 'c_lo', 'c_hi', 'glu_w', 'wp_lo', 'wp_hi', 'w_out')


def _prepare_and_sample(x, h_re, h_im, pool_prefix, w_in, lam_re, lam_im, log_dt, b_re, b_im, c_re, c_im, d,
                        glu_w, glu_b, pool_w, pool_scale, w_out, ln_g, ln_b):
    bsz = x.shape[0]
    n_cols = SSM_GROUPS * SSM_STATE
    log_dt_cols = jnp.broadcast_to(log_dt.astype(F32)[:, None], (SSM_GROUPS, SSM_STATE))
    row = lambda v: v.astype(F32).reshape(1, -1)
    rows = dict(d=row(d), glu_b=row(glu_b), pscale=row(pool_scale), ln_g=row(ln_g), ln_b=row(ln_b))
    bf = lambda shape: jax.ShapeDtypeStruct(shape, BF16)
    f32 = lambda shape: jax.ShapeDtypeStruct(shape, F32)
    prepared_shape = dict(w_s=bf((D_MODEL, SSM_WIDTH)), w_g=bf((D_MODEL, SSM_WIDTH + 2 * POOL_WIDTH)),
                          b_lo=bf((MXU_TILE, 2 * HALF_STATE)), b_hi=bf((MXU_TILE, 2 * HALF_STATE)),
                          a_re=f32((SUBLANES, n_cols)), a_im=f32((SUBLANES, n_cols)),
                          c_lo=bf((2 * HALF_STATE, MXU_TILE)), c_hi=bf((2 * HALF_STATE, MXU_TILE)),
                          glu_w=bf((SSM_WIDTH, SSM_WIDTH)), wp_lo=bf((MXU_TILE, MXU_TILE)),
                          wp_hi=bf((MXU_TILE, MXU_TILE)), w_out=bf((D_MODEL, D_MODEL)))
    assert tuple(prepared_shape) == _PREPARED_ORDER and len(_PREPARED_ORDER) == N_PREPARED
    raw = (lam_re.astype(F32).reshape(1, n_cols), lam_im.astype(F32).reshape(1, n_cols),
           log_dt_cols.reshape(1, n_cols), b_re.astype(F32).reshape(n_cols, SSM_GROUP_CH),
           b_im.astype(F32).reshape(n_cols, SSM_GROUP_CH),
           c_re.astype(F32).reshape(SSM_WIDTH, SSM_STATE), c_im.astype(F32).reshape(SSM_WIDTH, SSM_STATE),
           pool_w.astype(F32).reshape(POOL_WIDTH, POOL_GROUP_CH), w_in.astype(F32), glu_w.astype(F32),
           w_out.astype(F32))
    assert len(raw) == N_RAW
    prefix = jnp.swapaxes(pool_prefix.astype(F32), 0, 1)
    vmem = pl.BlockSpec(memory_space=pltpu.VMEM)
    hbm = pl.BlockSpec(memory_space=pl.ANY)
    outs = pl.pallas_call(
        _prep_sample_kernel,
        in_specs=[vmem] * (N_RAW + 5) + [hbm] + [vmem] * 3,
        out_specs=[vmem] * N_PREPARED + [hbm, vmem, vmem, vmem],
        out_shape=[prepared_shape[k] for k in _PREPARED_ORDER]
        + [f32((bsz, 1, D_MODEL)), f32((bsz, n_cols)), f32((bsz, n_cols)), f32((bsz, POOL_WIDTH))],
        scratch_shapes=[pltpu.VMEM((bsz, D_MODEL), F32),
                        pltpu.VMEM((bsz, D_MODEL), F32),
                        pltpu.VMEM((bsz, STATE_COLS), BF16),
                        pltpu.SemaphoreType.DMA((2,))],
        compiler_params=pltpu.CompilerParams(vmem_limit_bytes=VMEM_LIMIT_BYTES),
        name="prepare_weights_and_sample_layer",
    )(*raw, rows['d'], rows['glu_b'], rows['pscale'], rows['ln_g'], rows['ln_b'], x,
      h_re.astype(F32).reshape(bsz, n_cols), h_im.astype(F32).reshape(bsz, n_cols), prefix)
    w = dict(zip(_PREPARED_ORDER, outs[:N_PREPARED]))
    w.update(rows)
    y, n_re, n_im, p_in = outs[N_PREPARED:]
    state_shape = (bsz, SSM_GROUPS, SSM_STATE)
    buf = jnp.concatenate([pool_prefix.astype(F32)[:, 1:], p_in[:, None, :]], axis=1)
    return w, y, n_re.reshape(state_shape), n_im.reshape(state_shape), buf


_WEIGHT_ORDER = ('w_s', 'w_g', 'b_lo', 'b_hi', 'a_re', 'a_im', 'c_lo', 'c_hi', 'd', 'glu_w', 'glu_b',
                 'wp_lo', 'wp_hi', 'pscale', 'w_out', 'ln_g', 'ln_b')


def _resident(shape):
    return pl.BlockSpec(shape, lambda s: (0,) * len(shape), pipeline_mode=pl.Buffered(1))


def _prompt_layer(x, w):
    bsz, seq, _ = x.shape
    assert bsz == SUBLANES and seq % CHUNK_TOKENS == 0 and seq >= X_SLOTS * CHUNK_TOKENS
    rows = CHUNK_TOKENS * SUBLANES
    halo = HALO_TOKENS * SUBLANES
    n_cols = SSM_GROUPS * SSM_STATE
    weights = [w[k] for k in _WEIGHT_ORDER]
    y, h_re, h_im, ptail = pl.pallas_call(
        _prompt_kernel,
        grid=(seq // CHUNK_TOKENS,),
        in_specs=[pl.BlockSpec(memory_space=pl.ANY)] + [_resident(a.shape) for a in weights],
        out_specs=[pl.BlockSpec(memory_space=pl.ANY),
                   pl.BlockSpec((SUBLANES, n_cols), lambda s: (0, 0)),
                   pl.BlockSpec((SUBLANES, n_cols), lambda s: (0, 0)),
                   pl.BlockSpec((halo, POOL_WIDTH), lambda s: (0, 0))],
        out_shape=[jax.ShapeDtypeStruct(x.shape, F32),
                   jax.ShapeDtypeStruct((SUBLANES, n_cols), F32),
                   jax.ShapeDtypeStruct((SUBLANES, n_cols), F32),
                   jax.ShapeDtypeStruct((halo, POOL_WIDTH), F32)],
        scratch_shapes=[pltpu.VMEM((X_SLOTS, CHUNK_TOKENS, SUBLANES, D_MODEL), F32),
                        pltpu.VMEM((2, CHUNK_TOKENS, SUBLANES, D_MODEL), F32),
                        pltpu.VMEM((rows, STATE_COLS), F32),
                        pltpu.VMEM((rows, STATE_COLS), BF16),
                        pltpu.VMEM((halo + rows, POOL_WIDTH), F32),
                        pltpu.SemaphoreType.DMA((X_SLOTS,)),
                        pltpu.SemaphoreType.DMA((2,))],
        compiler_params=pltpu.CompilerParams(dimension_semantics=("arbitrary",),
                                             vmem_limit_bytes=VMEM_LIMIT_BYTES),
        name="prompt_layer",
    )(x, *weights)
    buf = ptail.reshape(HALO_TOKENS, SUBLANES, POOL_WIDTH)[HALO_TOKENS - POOL_BUF:]
    state_shape = (bsz, SSM_GROUPS, SSM_STATE)
    return y, h_re.reshape(state_shape), h_im.reshape(state_shape), jnp.swapaxes(buf, 0, 1)


def kernel(x_prompt, x_sample, state_ssm_re, state_ssm_im, state_pool, w_in, ssm_lambda_re, ssm_lambda_im, ssm_log_dt, ssm_b_re, ssm_b_im, ssm_c_re, ssm_c_im, ssm_d, glu_w, glu_b, pool_w, pool_scale, w_out, ln_g, ln_b):
    out_dtype = x_prompt.dtype
    depth = w_in.shape[0]
    hp = x_prompt.astype(F32)
    hs = x_sample.astype(F32)
    p_re, p_im, p_buf, s_re, s_im, s_buf = [], [], [], [], [], []
    for l in range(depth):
        w, hs, hr, hi, buf = _prepare_and_sample(
            hs, state_ssm_re[l], state_ssm_im[l], state_pool[l], w_in[l], ssm_lambda_re[l], ssm_lambda_im[l],
            ssm_log_dt[l], ssm_b_re[l], ssm_b_im[l], ssm_c_re[l], ssm_c_im[l], ssm_d[l], glu_w[l], glu_b[l],
            pool_w[l], pool_scale[l], w_out[l], ln_g[l], ln_b[l])
        s_re.append(hr); s_im.append(hi); s_buf.append(buf)
        hp, hr, hi, buf = _prompt_layer(hp, w)
        p_re.append(hr); p_im.append(hi); p_buf.append(buf)
    return (hp.astype(out_dtype), hs.astype(out_dtype),
            jnp.stack(p_re), jnp.stack(p_im), jnp.stack(p_buf),
            jnp.stack(s_re), jnp.stack(s_im), jnp.stack(s_buf))
```

```python
import jax
import jax.numpy as jnp
from jax import lax
from jax.experimental import pallas as pl
from jax.experimental.pallas import tpu as pltpu

D_MODEL = 1024
SSM_WIDTH = 512
POOL_WIDTH = 512
SSM_GROUPS = 32
SSM_GROUP_CH = 16
SSM_STATE = 64
POOL_WINDOWS = (2, 4, 8, 16)
POOL_GROUP_CH = 128
POOL_BUF = 15
DN_ALPHA = 2.0 ** 0.25
LN_EPS = 1e-5

SUBLANES = 8
MXU_TILE = 256
HALF_GROUPS = MXU_TILE // SSM_GROUP_CH
HALF_STATE = HALF_GROUPS * SSM_STATE
LANES = 128
STATE_COLS = 4 * HALF_STATE
LANE_BLOCKS = HALF_STATE // LANES
CHUNK_TOKENS = 64
TAIL_BLOCK_TOKENS = (32, 32)
X_SLOTS = 3
HALO_TOKENS = 16
VMEM_LIMIT_BYTES = 48 * 1024 * 1024

F32 = jnp.float32
BF16 = jnp.bfloat16


def _dot(a, b):
    return jnp.dot(a.astype(BF16), b, preferred_element_type=F32)


def _state_col(half, block, im):
    return half * 2 * HALF_STATE + block * 2 * LANES + im * LANES


def _ssm_input(xb, w_s_ref, b_lo_ref, b_hi_ref):
    s_in = jnp.dot(xb, w_s_ref[...], preferred_element_type=F32)
    sb = s_in.astype(BF16)
    bu_lo = jnp.dot(sb[:, :MXU_TILE], b_lo_ref[...], preferred_element_type=F32)
    bu_hi = jnp.dot(sb[:, MXU_TILE:], b_hi_ref[...], preferred_element_type=F32)
    return s_in, bu_lo, bu_hi


def _layer_norm(r, ln_g_ref, ln_b_ref):
    mu = jnp.mean(r, axis=-1, keepdims=True)
    rc = r - mu
    var = jnp.mean(rc * rc, axis=-1, keepdims=True)
    return rc * lax.rsqrt(var + LN_EPS) * ln_g_ref[...] + ln_b_ref[...]


def _mix(x, xb, s_in, hb, pooled, block_rows, w_g_ref, c_lo_ref, c_hi_ref, d_ref, glu_w_ref,
         glu_b_ref, wp_lo_ref, wp_hi_ref, pscale_ref, w_out_ref):
    assert sum(block_rows) == x.shape[0]
    starts = [sum(block_rows[:i]) for i in range(len(block_rows))]
    rs = [slice(a, a + n) for a, n in zip(starts, block_rows)]
    sy = []
    for r in rs:
        y_lo = jnp.dot(hb[r, :2 * HALF_STATE], c_lo_ref[...], preferred_element_type=F32)
        y_hi = jnp.dot(hb[r, 2 * HALF_STATE:], c_hi_ref[...], preferred_element_type=F32)
        sy.append(jnp.concatenate([y_lo, y_hi], axis=-1) + d_ref[...] * s_in[r])
    s_gate = jnp.dot(xb, w_g_ref[:, :SSM_WIDTH], preferred_element_type=F32)
    sy = [jax.nn.gelu(v) for v in sy]
    glu = [_dot(v, glu_w_ref[...]) + glu_b_ref[...] for v in sy]
    p_gate = jnp.dot(xb, w_g_ref[:, SSM_WIDTH + POOL_WIDTH:], preferred_element_type=F32)
    pb = pooled.astype(BF16)
    py = [jnp.concatenate([jnp.dot(pb[r, :MXU_TILE], wp_lo_ref[...], preferred_element_type=F32),
                           jnp.dot(pb[r, MXU_TILE:], wp_hi_ref[...], preferred_element_type=F32)], axis=-1)
          * pscale_ref[...] for r in rs]
    mixed = [jnp.concatenate([sy[i] * jax.nn.sigmoid(glu[i]) * jax.nn.silu(s_gate[r]),
                              py[i] * jax.nn.silu(p_gate[r])], axis=-1) for i, r in enumerate(rs)]
    return [DN_ALPHA * x[r] + _dot(mixed[i], w_out_ref[...]) for i, r in enumerate(rs)]


def _in_copies(x_hbm, xbuf, sem, chunk, slot):
    return [pltpu.make_async_copy(x_hbm.at[b, pl.ds(chunk * CHUNK_TOKENS, CHUNK_TOKENS), :],
                                  xbuf.at[slot, :, b, :], sem.at[slot]) for b in range(SUBLANES)]


def _out_copies(ybuf, y_hbm, sem, chunk, slot):
    return [pltpu.make_async_copy(ybuf.at[slot, :, b, :],
                                  y_hbm.at[b, pl.ds(chunk * CHUNK_TOKENS, CHUNK_TOKENS), :],
                                  sem.at[slot]) for b in range(SUBLANES)]


def _prompt_kernel(x_hbm, w_s_ref, w_g_ref, b_lo_ref, b_hi_ref, a_re_ref, a_im_ref, c_lo_ref, c_hi_ref,
                   d_ref, glu_w_ref, glu_b_ref, wp_lo_ref, wp_hi_ref, pscale_ref, w_out_ref, ln_g_ref,
                   ln_b_ref, y_hbm, hre_ref, him_ref, ptail_ref, xbuf, ybuf, hbuf, hb16, pbuf, sem_in, sem_out):
    s = pl.program_id(0)
    n = pl.num_programs(0)
    xslot = lax.rem(s, X_SLOTS)
    yslot = lax.rem(s, 2)
    ahead = jnp.minimum(s + X_SLOTS - 1, n - 1)
    ahead_slot = lax.rem(s + X_SLOTS - 1, X_SLOTS)
    rows = CHUNK_TOKENS * SUBLANES
    halo = HALO_TOKENS * SUBLANES

    @pl.when(s == 0)
    def _():
        for c0 in range(X_SLOTS - 1):
            for c in _in_copies(x_hbm, xbuf, sem_in, c0, c0):
                c.start()
        hre_ref[...] = jnp.zeros_like(hre_ref)
        him_ref[...] = jnp.zeros_like(him_ref)
        pbuf[0:halo, :] = jnp.zeros((halo, POOL_WIDTH), F32)

    for c in _in_copies(x_hbm, xbuf, sem_in, s, xslot):
        c.wait()

    x = xbuf[xslot].reshape(rows, D_MODEL)
    xb = x.astype(BF16)
    s_in, bu_lo, bu_hi = _ssm_input(xb, w_s_ref, b_lo_ref, b_hi_ref)
    hbuf[:, :2 * HALF_STATE] = bu_lo
    hbuf[:, 2 * HALF_STATE:] = bu_hi

    p_in = jnp.dot(xb, w_g_ref[:, SSM_WIDTH:SSM_WIDTH + POOL_WIDTH], preferred_element_type=F32)
    pbuf[halo:halo + rows, :] = p_in

    for k in range(2):
        for p in range(LANE_BLOCKS):
            c_re, c_im = _state_col(k, p, 0), _state_col(k, p, 1)
            a_col = k * HALF_STATE + p * LANES
            ar = a_re_ref[:, a_col:a_col + LANES]
            ai = a_im_ref[:, a_col:a_col + LANES]
            hr = hre_ref[:, a_col:a_col + LANES]
            hi = him_ref[:, a_col:a_col + LANES]
            for t2 in range(CHUNK_TOKENS // 2):
                pair = []
                for t in (2 * t2, 2 * t2 + 1):
                    r = t * SUBLANES
                    hr, hi = (ar * hr - ai * hi + hbuf[r:r + SUBLANES, c_re:c_re + LANES],
                              ar * hi + ai * hr + hbuf[r:r + SUBLANES, c_im:c_im + LANES])
                    pair.append((hr, hi))
                r2 = 2 * t2 * SUBLANES
                hb16[r2:r2 + 2 * SUBLANES, c_re:c_re + LANES] = jnp.concatenate(
                    [pair[0][0], pair[1][0]], axis=0).astype(BF16)
                hb16[r2:r2 + 2 * SUBLANES, c_im:c_im + LANES] = jnp.concatenate(
                    [pair[0][1], pair[1][1]], axis=0).astype(BF16)
            hre_ref[:, a_col:a_col + LANES] = hr
            him_ref[:, a_col:a_col + LANES] = hi

    row = lax.broadcasted_iota(jnp.int32, (rows, POOL_GROUP_CH), 0)
    pos1 = s * CHUNK_TOKENS + lax.shift_right_logical(row, 3) + 1
    pooled = []
    for g, w in enumerate(POOL_WINDOWS):
        acc = pbuf[:, g * POOL_GROUP_CH:(g + 1) * POOL_GROUP_CH]
        shift = SUBLANES
        while shift < w * SUBLANES:
            acc = acc[shift:, :] + acc[:-shift, :]
            shift *= 2
        acc = acc[acc.shape[0] - rows:, :]
        cnt = jnp.minimum(pos1, w).astype(F32)
        pooled.append(acc / cnt - p_in[:, g * POOL_GROUP_CH:(g + 1) * POOL_GROUP_CH])
    pooled = jnp.concatenate(pooled, axis=-1)
    pbuf[0:halo, :] = pbuf[rows:rows + halo, :]
    ptail_ref[...] = pbuf[0:halo, :]

    res = _mix(x, xb, s_in, hb16, pooled, [t * SUBLANES for t in TAIL_BLOCK_TOKENS], w_g_ref, c_lo_ref, c_hi_ref,
               d_ref, glu_w_ref, glu_b_ref, wp_lo_ref, wp_hi_ref, pscale_ref, w_out_ref)
    t0 = 0
    for blk_tokens, r in zip(TAIL_BLOCK_TOKENS, res):
        y = _layer_norm(r, ln_g_ref, ln_b_ref)
        ybuf[yslot, t0:t0 + blk_tokens] = y.reshape(blk_tokens, SUBLANES, D_MODEL)
        t0 += blk_tokens

    for c in _out_copies(ybuf, y_hbm, sem_out, s, yslot):
        c.start()
    for c in _in_copies(x_hbm, xbuf, sem_in, ahead, ahead_slot):
        c.start()

    @pl.when(s >= 1)
    def _():
        for c in _out_copies(ybuf, y_hbm, sem_out, s - 1, 1 - yslot):
            c.wait()

    @pl.when(s == n - 1)
    def _():
        for c in _out_copies(ybuf, y_hbm, sem_out, s, yslot):
            c.wait()
        for back in range(X_SLOTS - 1):
            for c in _in_copies(x_hbm, xbuf, sem_in, n - 1, lax.rem(s + X_SLOTS - 1 - back, X_SLOTS)):
                c.wait()


def _sample_body(xs, h0_re_ref, h0_im_ref, prefix_ref, w_s_ref, w_g_ref, b_lo_ref, b_hi_ref, a_re_ref,
                 a_im_ref, c_lo_ref, c_hi_ref, d_ref, glu_w_ref, glu_b_ref, wp_lo_ref, wp_hi_ref, pscale_ref,
                 w_out_ref, ln_g_ref, ln_b_ref, ys, hre_ref, him_ref, pin_ref, hb16):
    x = xs[...]
    xb = x.astype(BF16)
    s_in, bu_lo, bu_hi = _ssm_input(xb, w_s_ref, b_lo_ref, b_hi_ref)
    bu = (bu_lo, bu_hi)
    for k in range(2):
        for p in range(LANE_BLOCKS):
            c_re, c_im = _state_col(k, p, 0), _state_col(k, p, 1)
            a_col = k * HALF_STATE + p * LANES
            ar = a_re_ref[0:1, a_col:a_col + LANES]
            ai = a_im_ref[0:1, a_col:a_col + LANES]
            hr = h0_re_ref[:, a_col:a_col + LANES]
            hi = h0_im_ref[:, a_col:a_col + LANES]
            b_re, b_im = _state_col(0, p, 0), _state_col(0, p, 1)
            nhr = ar * hr - ai * hi + bu[k][:, b_re:b_re + LANES]
            nhi = ar * hi + ai * hr + bu[k][:, b_im:b_im + LANES]
            hre_ref[:, a_col:a_col + LANES] = nhr
            him_ref[:, a_col:a_col + LANES] = nhi
            hb16[:, c_re:c_re + LANES] = nhr.astype(BF16)
            hb16[:, c_im:c_im + LANES] = nhi.astype(BF16)

    p_in = jnp.dot(xb, w_g_ref[:, SSM_WIDTH:SSM_WIDTH + POOL_WIDTH], preferred_element_type=F32)
    pin_ref[...] = p_in
    pooled = []
    for g, w in enumerate(POOL_WINDOWS):
        cols = slice(g * POOL_GROUP_CH, (g + 1) * POOL_GROUP_CH)
        acc = p_in[:, cols]
        for j in range(1, w):
            acc = acc + prefix_ref[POOL_BUF - j, :, cols]
        pooled.append(acc / float(w) - p_in[:, cols])
    pooled = jnp.concatenate(pooled, axis=-1)
    res = _mix(x, xb, s_in, hb16, pooled, [x.shape[0]], w_g_ref, c_lo_ref, c_hi_ref, d_ref, glu_w_ref, glu_b_ref,
               wp_lo_ref, wp_hi_ref, pscale_ref, w_out_ref)[0]
    ys[...] = _layer_norm(res, ln_g_ref, ln_b_ref)


def _group_mask(shape, row_group, col_group):
    r = lax.broadcasted_iota(jnp.int32, shape, 0) // row_group
    c = lax.broadcasted_iota(jnp.int32, shape, 1) // col_group
    return r == c


def _lane_tiler(width, copies):
    shape = (width, width * copies)
    r = lax.broadcasted_iota(jnp.int32, shape, 0)
    c = lax.broadcasted_iota(jnp.int32, shape, 1) % width
    return (r == c).astype(BF16)


def _prep_small(lam_re_ref, lam_im_ref, log_dt_ref, b_re_ref, b_im_ref, c_re_ref, c_im_ref, pw_ref,
                b_lo, b_hi, a_re, a_im, c_lo, c_hi, wp_lo, wp_hi):
    lam_re = lam_re_ref[...]
    lam_im = lam_im_ref[...]
    dt = jnp.exp(log_dt_ref[...])
    mag = jnp.exp(lam_re * dt)
    ar, ai = mag * jnp.cos(lam_im * dt), mag * jnp.sin(lam_im * dt)
    den = lam_re * lam_re + lam_im * lam_im
    nr = ar - 1.0
    qr = (nr * lam_re + ai * lam_im) / den
    qi = (ai * lam_re - nr * lam_im) / den
    n_cols = SSM_GROUPS * SSM_STATE
    a_re[...] = jnp.broadcast_to(ar, (SUBLANES, n_cols))
    a_im[...] = jnp.broadcast_to(ai, (SUBLANES, n_cols))

    tile_c = _lane_tiler(SSM_GROUP_CH, HALF_GROUPS)
    tile_n = _lane_tiler(SSM_STATE, HALF_GROUPS)
    mask_b = _group_mask((HALF_STATE, MXU_TILE), SSM_STATE, SSM_GROUP_CH)
    mask_c = _group_mask((MXU_TILE, HALF_STATE), SSM_GROUP_CH, SSM_STATE)
    for k, (b_out, c_out) in enumerate(((b_lo, c_lo), (b_hi, c_hi))):
        rows = slice(k * HALF_STATE, (k + 1) * HALF_STATE)
        q_re = jnp.broadcast_to(qr[:, rows], (128, HALF_STATE)).T[:, :SSM_GROUP_CH]
        q_im = jnp.broadcast_to(qi[:, rows], (128, HALF_STATE)).T[:, :SSM_GROUP_CH]
        b_re = b_re_ref[rows, :]
        b_im = b_im_ref[rows, :]
        bbar = (q_re * b_re - q_im * b_im, q_re * b_im + q_im * b_re)
        for j, bb in enumerate(bbar):
            wide = jnp.dot(bb.astype(BF16), tile_c, preferred_element_type=F32)
            blk = jnp.where(mask_b, wide, 0.0).T.astype(BF16)
            for p in range(LANE_BLOCKS):
                col = _state_col(0, p, j)
                b_out[:, col:col + LANES] = blk[:, p * LANES:(p + 1) * LANES]
        crow = slice(k * MXU_TILE, (k + 1) * MXU_TILE)
        for j, (c_ref, sign) in enumerate(((c_re_ref, 1.0), (c_im_ref, -1.0))):
            wide = jnp.dot(c_ref[crow, :].astype(BF16), tile_n, preferred_element_type=F32)
            blk = jnp.where(mask_c, sign * wide, 0.0).T.astype(BF16)
            for p in range(LANE_BLOCKS):
                row = _state_col(0, p, j)
                c_out[row:row + LANES, :] = blk[p * LANES:(p + 1) * LANES, :]

    mask_p = _group_mask((MXU_TILE, MXU_TILE), POOL_GROUP_CH, POOL_GROUP_CH)
    for k, wp in enumerate((wp_lo, wp_hi)):
        pw = pw_ref[k * MXU_TILE:(k + 1) * MXU_TILE, :]
        wp[...] = jnp.where(mask_p, jnp.concatenate([pw, pw], axis=1), 0.0).astype(BF16)


N_SMALL = 8
N_BIG = 3
N_PREPARED = 12
N_LOADS = 5
N_STORES = 9


def _prep_sample_kernel(*refs):
    small = refs[:N_SMALL]
    w_in_hbm, glu_hbm, w_out_hbm = refs[N_SMALL:N_SMALL + N_BIG]
    base = N_SMALL + N_BIG
    d_ref, glu_b_ref, pscale_ref, ln_g_ref, ln_b_ref, x_hbm, h0_re_ref, h0_im_ref, prefix_hbm = refs[base:base + 9]
    prepared = refs[base + 9:base + 9 + N_PREPARED]
    (w_s_hbm, w_g_hbm, b_lo_hbm, b_hi_hbm, a_re, a_im, c_lo_hbm, c_hi_hbm, glu16_hbm, wp_lo, wp_hi,
     w_out16_hbm) = prepared
    (y_hbm, hre_ref, him_ref, pin_ref, w_in_f, glu_f, w_out_f, prefix_v, w_s16, w_g16, glu16, w_out16, b_lo, b_hi,
     c_lo, c_hi, xs, ys, hb16, sem_in, sem_out) = refs[base + 9 + N_PREPARED:]

    loads = [pltpu.make_async_copy(src, dst, sem_in.at[i]) for i, (src, dst) in enumerate(
        ((w_in_hbm, w_in_f), (w_out_hbm, w_out_f), (glu_hbm, glu_f), (prefix_hbm, prefix_v),
         (x_hbm.at[:, 0, :], xs)))]
    for c in loads:
        c.start()
    stores = []

    def send(src, dst):
        c = pltpu.make_async_copy(src, dst, sem_out.at[len(stores)])
        c.start()
        stores.append(c)

    _prep_small(*small, b_lo, b_hi, a_re, a_im, c_lo, c_hi, wp_lo, wp_hi)
    for src, dst in ((b_lo, b_lo_hbm), (b_hi, b_hi_hbm), (c_lo, c_lo_hbm), (c_hi, c_hi_hbm)):
        send(src, dst)
    loads[0].wait()
    w_s16[...] = w_in_f[:, :SSM_WIDTH].astype(BF16)
    w_g16[...] = w_in_f[:, SSM_WIDTH:].astype(BF16)
    send(w_s16, w_s_hbm)
    send(w_g16, w_g_hbm)
    loads[1].wait()
    w_out16[...] = w_out_f[...].astype(BF16)
    send(w_out16, w_out16_hbm)
    loads[2].wait()
    glu16[...] = glu_f[...].astype(BF16)
    send(glu16, glu16_hbm)
    loads[3].wait()
    loads[4].wait()
    _sample_body(xs, h0_re_ref, h0_im_ref, prefix_v, w_s16, w_g16, b_lo, b_hi, a_re, a_im, c_lo, c_hi, d_ref,
                 glu16, glu_b_ref, wp_lo, wp_hi, pscale_ref, w_out16, ln_g_ref, ln_b_ref, ys, hre_ref, him_ref,
                 pin_ref, hb16)
    send(ys, y_hbm.at[:, 0, :])
    assert len(stores) == N_STORES
    for c in stores:
        c.wait()


_PREPARED_ORDER = ('w_s', 'w_g', 'b_lo', 'b_hi', 'a_re', 'a_im', 'c_lo', 'c_hi', 'glu_w', 'wp_lo', 'wp_hi', 'w_out')


def _prepare_and_sample(x, h_re, h_im, pool_prefix, w_in, lam_re, lam_im, log_dt, b_re, b_im, c_re, c_im, d,
                        glu_w, glu_b, pool_w, pool_scale, w_out, ln_g, ln_b):
    bsz = x.shape[0]
    n_cols = SSM_GROUPS * SSM_STATE
    log_dt_cols = jnp.broadcast_to(log_dt.astype(F32)[:, None], (SSM_GROUPS, SSM_STATE))
    row = lambda v: v.astype(F32).reshape(1, -1)
    rows = dict(d=row(d), glu_b=row(glu_b), pscale=row(pool_scale), ln_g=row(ln_g), ln_b=row(ln_b))
    bf = lambda shape: jax.ShapeDtypeStruct(shape, BF16)
    f32 = lambda shape: jax.ShapeDtypeStruct(shape, F32)
    prepared_shape = dict(w_s=bf((D_MODEL, SSM_WIDTH)), w_g=bf((D_MODEL, SSM_WIDTH + 2 * POOL_WIDTH)),
                          b_lo=bf((MXU_TILE, 2 * HALF_STATE)), b_hi=bf((MXU_TILE, 2 * HALF_STATE)),
                          a_re=f32((SUBLANES, n_cols)), a_im=f32((SUBLANES, n_cols)),
                          c_lo=bf((2 * HALF_STATE, MXU_TILE)), c_hi=bf((2 * HALF_STATE, MXU_TILE)),
                          glu_w=bf((SSM_WIDTH, SSM_WIDTH)), wp_lo=bf((MXU_TILE, MXU_TILE)),
                          wp_hi=bf((MXU_TILE, MXU_TILE)), w_out=bf((D_MODEL, D_MODEL)))
    assert tuple(prepared_shape) == _PREPARED_ORDER and len(_PREPARED_ORDER) == N_PREPARED
    small = (lam_re.astype(F32).reshape(1, n_cols), lam_im.astype(F32).reshape(1, n_cols),
             log_dt_cols.reshape(1, n_cols), b_re.astype(F32).reshape(n_cols, SSM_GROUP_CH),
             b_im.astype(F32).reshape(n_cols, SSM_GROUP_CH),
             c_re.astype(F32).reshape(SSM_WIDTH, SSM_STATE), c_im.astype(F32).reshape(SSM_WIDTH, SSM_STATE),
             pool_w.astype(F32).reshape(POOL_WIDTH, POOL_GROUP_CH))
    big = (w_in.astype(F32), glu_w.astype(F32), w_out.astype(F32))
    assert len(small) == N_SMALL and len(big) == N_BIG
    prefix = jnp.swapaxes(pool_prefix.astype(F32), 0, 1)
    vmem = pl.BlockSpec(memory_space=pltpu.VMEM)
    hbm = pl.BlockSpec(memory_space=pl.ANY)
    in_vmem = ('a_re', 'a_im', 'wp_lo', 'wp_hi')
    staged = [k for k in _PREPARED_ORDER if k not in in_vmem]
    outs = pl.pallas_call(
        _prep_sample_kernel,
        in_specs=[vmem] * N_SMALL + [hbm] * N_BIG + [vmem] * 5 + [hbm, vmem, vmem, hbm],
        out_specs=[vmem if k in in_vmem else hbm for k in _PREPARED_ORDER] + [hbm, vmem, vmem, vmem],
        out_shape=[prepared_shape[k] for k in _PREPARED_ORDER]
        + [f32((bsz, 1, D_MODEL)), f32((bsz, n_cols)), f32((bsz, n_cols)), f32((bsz, POOL_WIDTH))],
        scratch_shapes=[pltpu.VMEM(w.shape, F32) for w in (big[0], big[1], big[2], prefix)]
        + [pltpu.VMEM(prepared_shape[k].shape, BF16)
           for k in ('w_s', 'w_g', 'glu_w', 'w_out', 'b_lo', 'b_hi', 'c_lo', 'c_hi')]
        + [pltpu.VMEM((bsz, D_MODEL), F32),
           pltpu.VMEM((bsz, D_MODEL), F32),
           pltpu.VMEM((bsz, STATE_COLS), BF16),
           pltpu.SemaphoreType.DMA((N_LOADS,)),
           pltpu.SemaphoreType.DMA((N_STORES,))],
        compiler_params=pltpu.CompilerParams(vmem_limit_bytes=VMEM_LIMIT_BYTES),
        name="prepare_weights_and_sample_layer",
    )(*small, *big, rows['d'], rows['glu_b'], rows['pscale'], rows['ln_g'], rows['ln_b'], x,
      h_re.astype(F32).reshape(bsz, n_cols), h_im.astype(F32).reshape(bsz, n_cols), prefix)
    assert len(staged) == N_STORES - 1
    w = dict(zip(_PREPARED_ORDER, outs[:N_PREPARED]))
    w.update(rows)
    y, n_re, n_im, p_in = outs[N_PREPARED:]
    state_shape = (bsz, SSM_GROUPS, SSM_STATE)
    buf = jnp.concatenate([pool_prefix.astype(F32)[:, 1:], p_in[:, None, :]], axis=1)
    return w, y, n_re.reshape(state_shape), n_im.reshape(state_shape), buf


_WEIGHT_ORDER = ('w_s', 'w_g', 'b_lo', 'b_hi', 'a_re', 'a_im', 'c_lo', 'c_hi', 'd', 'glu_w', 'glu_b',
                 'wp_lo', 'wp_hi', 'pscale', 'w_out', 'ln_g', 'ln_b')


def _resident(shape):
    return pl.BlockSpec(shape, lambda s: (0,) * len(shape), pipeline_mode=pl.Buffered(1))


def _prompt_layer(x, w):
    bsz, seq, _ = x.shape
    assert bsz == SUBLANES and seq % CHUNK_TOKENS == 0 and seq >= X_SLOTS * CHUNK_TOKENS
    rows = CHUNK_TOKENS * SUBLANES
    halo = HALO_TOKENS * SUBLANES
    n_cols = SSM_GROUPS * SSM_STATE
    weights = [w[k] for k in _WEIGHT_ORDER]
    y, h_re, h_im, ptail = pl.pallas_call(
        _prompt_kernel,
        grid=(seq // CHUNK_TOKENS,),
        in_specs=[pl.BlockSpec(memory_space=pl.ANY)] + [_resident(a.shape) for a in weights],
        out_specs=[pl.BlockSpec(memory_space=pl.ANY),
                   pl.BlockSpec((SUBLANES, n_cols), lambda s: (0, 0)),
                   pl.BlockSpec((SUBLANES, n_cols), lambda s: (0, 0)),
                   pl.BlockSpec((halo, POOL_WIDTH), lambda s: (0, 0))],
        out_shape=[jax.ShapeDtypeStruct(x.shape, F32),
                   jax.ShapeDtypeStruct((SUBLANES, n_cols), F32),
                   jax.ShapeDtypeStruct((SUBLANES, n_cols), F32),
                   jax.ShapeDtypeStruct((halo, POOL_WIDTH), F32)],
        scratch_shapes=[pltpu.VMEM((X_SLOTS, CHUNK_TOKENS, SUBLANES, D_MODEL), F32),
                        pltpu.VMEM((2, CHUNK_TOKENS, SUBLANES, D_MODEL), F32),
                        pltpu.VMEM((rows, STATE_COLS), F32),
                        pltpu.VMEM((rows, STATE_COLS), BF16),
                        pltpu.VMEM((halo + rows, POOL_WIDTH), F32),
                        pltpu.SemaphoreType.DMA((X_SLOTS,)),
                        pltpu.SemaphoreType.DMA((2,))],
        compiler_params=pltpu.CompilerParams(dimension_semantics=("arbitrary",),
                                             vmem_limit_bytes=VMEM_LIMIT_BYTES),
        name="prompt_layer",
    )(x, *weights)
    buf = ptail.reshape(HALO_TOKENS, SUBLANES, POOL_WIDTH)[HALO_TOKENS - POOL_BUF:]
    state_shape = (bsz, SSM_GROUPS, SSM_STATE)
    return y, h_re.reshape(state_shape), h_im.reshape(state_shape), jnp.swapaxes(buf, 0, 1)


def kernel(x_prompt, x_sample, state_ssm_re, state_ssm_im, state_pool, w_in, ssm_lambda_re, ssm_lambda_im, ssm_log_dt, ssm_b_re, ssm_b_im, ssm_c_re, ssm_c_im, ssm_d, glu_w, glu_b, pool_w, pool_scale, w_out, ln_g, ln_b):
    out_dtype = x_prompt.dtype
    depth = w_in.shape[0]
    hp = x_prompt.astype(F32)
    hs = x_sample.astype(F32)
    p_re, p_im, p_buf, s_re, s_im, s_buf = [], [], [], [], [], []
    for l in range(depth):
        w, hs, hr, hi, buf = _prepare_and_sample(
            hs, state_ssm_re[l], state_ssm_im[l], state_pool[l], w_in[l], ssm_lambda_re[l], ssm_lambda_im[l],
            ssm_log_dt[l], ssm_b_re[l], ssm_b_im[l], ssm_c_re[l], ssm_c_im[l], ssm_d[l], glu_w[l], glu_b[l],
            pool_w[l], pool_scale[l], w_out[l], ln_g[l], ln_b[l])
        s_re.append(hr); s_im.append(hi); s_buf.append(buf)
        hp, hr, hi, buf = _prompt_layer(hp, w)
        p_re.append(hr); p_im.append(hi); p_buf.append(buf)
    return (hp.astype(out_dtype), hs.astype(out_dtype),
            jnp.stack(p_re), jnp.stack(p_im), jnp.stack(p_buf),
            jnp.stack(s_re), jnp.stack(s_im), jnp.stack(s_buf))
```

```python
import jax
import jax.numpy as jnp
from jax import lax
from jax.experimental import pallas as pl
from jax.experimental.pallas import tpu as pltpu

D_MODEL = 1024
SSM_WIDTH = 512
POOL_WIDTH = 512
SSM_GROUPS = 32
SSM_GROUP_CH = 16
SSM_STATE = 64
POOL_WINDOWS = (2, 4, 8, 16)
POOL_GROUP_CH = 128
POOL_BUF = 15
DN_ALPHA = 2.0 ** 0.25
LN_EPS = 1e-5

SUBLANES = 8
MXU_TILE = 256
HALF_GROUPS = MXU_TILE // SSM_GROUP_CH
HALF_STATE = HALF_GROUPS * SSM_STATE
LANES = 128
STATE_COLS = 4 * HALF_STATE
LANE_BLOCKS = HALF_STATE // LANES
CHUNK_TOKENS = 64
TAIL_BLOCK_TOKENS = (32, 32)
X_SLOTS = 3
HALO_TOKENS = 16
VMEM_LIMIT_BYTES = 48 * 1024 * 1024

F32 = jnp.float32
BF16 = jnp.bfloat16


def _dot(a, b):
    return jnp.dot(a.astype(BF16), b, preferred_element_type=F32)


def _state_col(half, block, im):
    return half * 2 * HALF_STATE + block * 2 * LANES + im * LANES


def _ssm_input(xb, w_s_ref, b_lo_ref, b_hi_ref):
    s_in = jnp.dot(xb, w_s_ref[...], preferred_element_type=F32)
    sb = s_in.astype(BF16)
    bu_lo = jnp.dot(sb[:, :MXU_TILE], b_lo_ref[...], preferred_element_type=F32)
    bu_hi = jnp.dot(sb[:, MXU_TILE:], b_hi_ref[...], preferred_element_type=F32)
    return s_in, bu_lo, bu_hi


def _layer_norm(r, ln_g_ref, ln_b_ref):
    mu = jnp.mean(r, axis=-1, keepdims=True)
    rc = r - mu
    var = jnp.mean(rc * rc, axis=-1, keepdims=True)
    return rc * lax.rsqrt(var + LN_EPS) * ln_g_ref[...] + ln_b_ref[...]


def _mix(x, xb, s_in, hb, pooled, block_rows, w_g_ref, c_lo_ref, c_hi_ref, d_ref, glu_w_ref,
         glu_b_ref, wp_lo_ref, wp_hi_ref, pscale_ref, w_out_ref):
    assert sum(block_rows) == x.shape[0]
    starts = [sum(block_rows[:i]) for i in range(len(block_rows))]
    rs = [slice(a, a + n) for a, n in zip(starts, block_rows)]
    sy = []
    for r in rs:
        y_lo = jnp.dot(hb[r, :2 * HALF_STATE], c_lo_ref[...], preferred_element_type=F32)
        y_hi = jnp.dot(hb[r, 2 * HALF_STATE:], c_hi_ref[...], preferred_element_type=F32)
        sy.append(jnp.concatenate([y_lo, y_hi], axis=-1) + d_ref[...] * s_in[r])
    s_gate = jnp.dot(xb, w_g_ref[:, :SSM_WIDTH], preferred_element_type=F32)
    sy = [jax.nn.gelu(v) for v in sy]
    glu = [_dot(v, glu_w_ref[...]) + glu_b_ref[...] for v in sy]
    p_gate = jnp.dot(xb, w_g_ref[:, SSM_WIDTH + POOL_WIDTH:], preferred_element_type=F32)
    pb = pooled.astype(BF16)
    py = [jnp.concatenate([jnp.dot(pb[r, :MXU_TILE], wp_lo_ref[...], preferred_element_type=F32),
                           jnp.dot(pb[r, MXU_TILE:], wp_hi_ref[...], preferred_element_type=F32)], axis=-1)
          * pscale_ref[...] for r in rs]
    mixed = [jnp.concatenate([sy[i] * jax.nn.sigmoid(glu[i]) * jax.nn.silu(s_gate[r]),
                              py[i] * jax.nn.silu(p_gate[r])], axis=-1) for i, r in enumerate(rs)]
    return [DN_ALPHA * x[r] + _dot(mixed[i], w_out_ref[...]) for i, r in enumerate(rs)]


def _in_copies(x_hbm, xbuf, sem, chunk, slot):
    return [pltpu.make_async_copy(x_hbm.at[b, pl.ds(chunk * CHUNK_TOKENS, CHUNK_TOKENS), :],
                                  xbuf.at[slot, :, b, :], sem.at[slot]) for b in range(SUBLANES)]


def _out_copies(ybuf, y_hbm, sem, chunk, slot):
    return [pltpu.make_async_copy(ybuf.at[slot, :, b, :],
                                  y_hbm.at[b, pl.ds(chunk * CHUNK_TOKENS, CHUNK_TOKENS), :],
                                  sem.at[slot]) for b in range(SUBLANES)]


def _prompt_kernel(x_hbm, w_s_ref, w_g_ref, b_lo_ref, b_hi_ref, a_re_ref, a_im_ref, c_lo_ref, c_hi_ref,
                   d_ref, glu_w_ref, glu_b_ref, wp_lo_ref, wp_hi_ref, pscale_ref, w_out_ref, ln_g_ref,
                   ln_b_ref, y_hbm, hre_ref, him_ref, ptail_ref, xbuf, ybuf, hbuf, hb16, pbuf, sem_in, sem_out):
    s = pl.program_id(0)
    n = pl.num_programs(0)
    xslot = lax.rem(s, X_SLOTS)
    yslot = lax.rem(s, 2)
    ahead = jnp.minimum(s + X_SLOTS - 1, n - 1)
    ahead_slot = lax.rem(s + X_SLOTS - 1, X_SLOTS)
    rows = CHUNK_TOKENS * SUBLANES
    halo = HALO_TOKENS * SUBLANES

    @pl.when(s == 0)
    def _():
        for c0 in range(X_SLOTS - 1):
            for c in _in_copies(x_hbm, xbuf, sem_in, c0, c0):
                c.start()
        hre_ref[...] = jnp.zeros_like(hre_ref)
        him_ref[...] = jnp.zeros_like(him_ref)
        pbuf[0:halo, :] = jnp.zeros((halo, POOL_WIDTH), F32)

    for c in _in_copies(x_hbm, xbuf, sem_in, s, xslot):
        c.wait()

    x = xbuf[xslot].reshape(rows, D_MODEL)
    xb = x.astype(BF16)
    s_in, bu_lo, bu_hi = _ssm_input(xb, w_s_ref, b_lo_ref, b_hi_ref)
    hbuf[:, :2 * HALF_STATE] = bu_lo
    hbuf[:, 2 * HALF_STATE:] = bu_hi

    p_in = jnp.dot(xb, w_g_ref[:, SSM_WIDTH:SSM_WIDTH + POOL_WIDTH], preferred_element_type=F32)
    pbuf[halo:halo + rows, :] = p_in

    for k in range(2):
        for p in range(LANE_BLOCKS):
            c_re, c_im = _state_col(k, p, 0), _state_col(k, p, 1)
            a_col = k * HALF_STATE + p * LANES
            ar = a_re_ref[:, a_col:a_col + LANES]
            ai = a_im_ref[:, a_col:a_col + LANES]
            hr = hre_ref[:, a_col:a_col + LANES]
            hi = him_ref[:, a_col:a_col + LANES]
            for t2 in range(CHUNK_TOKENS // 2):
                pair = []
                for t in (2 * t2, 2 * t2 + 1):
                    r = t * SUBLANES
                    hr, hi = (ar * hr - ai * hi + hbuf[r:r + SUBLANES, c_re:c_re + LANES],
                              ar * hi + ai * hr + hbuf[r:r + SUBLANES, c_im:c_im + LANES])
                    pair.append((hr, hi))
                r2 = 2 * t2 * SUBLANES
                hb16[r2:r2 + 2 * SUBLANES, c_re:c_re + LANES] = jnp.concatenate(
                    [pair[0][0], pair[1][0]], axis=0).astype(BF16)
                hb16[r2:r2 + 2 * SUBLANES, c_im:c_im + LANES] = jnp.concatenate(
                    [pair[0][1], pair[1][1]], axis=0).astype(BF16)
            hre_ref[:, a_col:a_col + LANES] = hr
            him_ref[:, a_col:a_col + LANES] = hi

    row = lax.broadcasted_iota(jnp.int32, (rows, POOL_GROUP_CH), 0)
    pos1 = s * CHUNK_TOKENS + lax.shift_right_logical(row, 3) + 1
    pooled = []
    for g, w in enumerate(POOL_WINDOWS):
        acc = pbuf[:, g * POOL_GROUP_CH:(g + 1) * POOL_GROUP_CH]
        shift = SUBLANES
        while shift < w * SUBLANES:
            acc = acc[shift:, :] + acc[:-shift, :]
            shift *= 2
        acc = acc[acc.shape[0] - rows:, :]
        cnt = jnp.minimum(pos1, w).astype(F32)
        pooled.append(acc / cnt - p_in[:, g * POOL_GROUP_CH:(g + 1) * POOL_GROUP_CH])
    pooled = jnp.concatenate(pooled, axis=-1)
    pbuf[0:halo, :] = pbuf[rows:rows + halo, :]
    ptail_ref[...] = pbuf[0:halo, :]

    res = _mix(x, xb, s_in, hb16, pooled, [t * SUBLANES for t in TAIL_BLOCK_TOKENS], w_g_ref, c_lo_ref, c_hi_ref,
               d_ref, glu_w_ref, glu_b_ref, wp_lo_ref, wp_hi_ref, pscale_ref, w_out_ref)
    t0 = 0
    for blk_tokens, r in zip(TAIL_BLOCK_TOKENS, res):
        y = _layer_norm(r, ln_g_ref, ln_b_ref)
        ybuf[yslot, t0:t0 + blk_tokens] = y.reshape(blk_tokens, SUBLANES, D_MODEL)
        t0 += blk_tokens

    for c in _out_copies(ybuf, y_hbm, sem_out, s, yslot):
        c.start()
    for c in _in_copies(x_hbm, xbuf, sem_in, ahead, ahead_slot):
        c.start()

    @pl.when(s >= 1)
    def _():
        for c in _out_copies(ybuf, y_hbm, sem_out, s - 1, 1 - yslot):
            c.wait()

    @pl.when(s == n - 1)
    def _():
        for c in _out_copies(ybuf, y_hbm, sem_out, s, yslot):
            c.wait()
        for back in range(X_SLOTS - 1):
            for c in _in_copies(x_hbm, xbuf, sem_in, n - 1, lax.rem(s + X_SLOTS - 1 - back, X_SLOTS)):
                c.wait()


def _sample_body(xs, h0_re_t, h0_im_t, prefix_ref, w_s_ref, w_g_ref, b_lo_ref, b_hi_ref, a_re_ref,
                 a_im_ref, c_lo_ref, c_hi_ref, d_ref, glu_w_ref, glu_b_ref, wp_lo_ref, wp_hi_ref, pscale_ref,
                 w_out_ref, ln_g_ref, ln_b_ref, ys, hre_t, him_t, pin_ref, hb16):
    x = xs[...]
    xb = x.astype(BF16)
    s_in, bu_lo, bu_hi = _ssm_input(xb, w_s_ref, b_lo_ref, b_hi_ref)
    bu = (bu_lo, bu_hi)
    for k in range(2):
        for p in range(LANE_BLOCKS):
            c_re, c_im = _state_col(k, p, 0), _state_col(k, p, 1)
            a_col = k * HALF_STATE + p * LANES
            ar = a_re_ref[0:1, a_col:a_col + LANES]
            ai = a_im_ref[0:1, a_col:a_col + LANES]
            hr = h0_re_t[a_col:a_col + LANES, :].T
            hi = h0_im_t[a_col:a_col + LANES, :].T
            b_re, b_im = _state_col(0, p, 0), _state_col(0, p, 1)
            nhr = ar * hr - ai * hi + bu[k][:, b_re:b_re + LANES]
            nhi = ar * hi + ai * hr + bu[k][:, b_im:b_im + LANES]
            hre_t[a_col:a_col + LANES, :] = nhr.T
            him_t[a_col:a_col + LANES, :] = nhi.T
            hb16[:, c_re:c_re + LANES] = nhr.astype(BF16)
            hb16[:, c_im:c_im + LANES] = nhi.astype(BF16)

    p_in = jnp.dot(xb, w_g_ref[:, SSM_WIDTH:SSM_WIDTH + POOL_WIDTH], preferred_element_type=F32)
    pin_ref[...] = p_in
    pooled = []
    for g, w in enumerate(POOL_WINDOWS):
        cols = slice(g * POOL_GROUP_CH, (g + 1) * POOL_GROUP_CH)
        acc = p_in[:, cols]
        for j in range(1, w):
            acc = acc + prefix_ref[POOL_BUF - j, :, cols]
        pooled.append(acc / float(w) - p_in[:, cols])
    pooled = jnp.concatenate(pooled, axis=-1)
    res = _mix(x, xb, s_in, hb16, pooled, [x.shape[0]], w_g_ref, c_lo_ref, c_hi_ref, d_ref, glu_w_ref, glu_b_ref,
               wp_lo_ref, wp_hi_ref, pscale_ref, w_out_ref)[0]
    ys[...] = _layer_norm(res, ln_g_ref, ln_b_ref)


def _group_mask(shape, row_group, col_group):
    r = lax.broadcasted_iota(jnp.int32, shape, 0) // row_group
    c = lax.broadcasted_iota(jnp.int32, shape, 1) // col_group
    return r == c


def _lane_tiler(width, copies):
    shape = (width, width * copies)
    r = lax.broadcasted_iota(jnp.int32, shape, 0)
    c = lax.broadcasted_iota(jnp.int32, shape, 1) % width
    return (r == c).astype(BF16)


def _discretise(lam_re, lam_im, log_dt):
    dt = jnp.exp(log_dt)
    mag = jnp.exp(lam_re * dt)
    ar, ai = mag * jnp.cos(lam_im * dt), mag * jnp.sin(lam_im * dt)
    den = lam_re * lam_re + lam_im * lam_im
    nr = ar - 1.0
    qr = (nr * lam_re + ai * lam_im) / den
    qi = (ai * lam_re - nr * lam_im) / den
    return ar, ai, qr, qi


def _prep_small(lam_re_row, lam_im_row, log_dt_row, lam_re_gn, lam_im_gn, log_dt_gn, b_re_ref, b_im_ref, c_re_ref,
                c_im_ref, pw_ref, b_lo, b_hi, a_re, a_im, c_lo, c_hi, wp_lo, wp_hi):
    ar, ai, _, _ = _discretise(lam_re_row[...], lam_im_row[...], log_dt_row[...])
    n_cols = SSM_GROUPS * SSM_STATE
    a_re[...] = jnp.broadcast_to(ar, (SUBLANES, n_cols))
    a_im[...] = jnp.broadcast_to(ai, (SUBLANES, n_cols))
    _, _, qr, qi = _discretise(lam_re_gn[...], lam_im_gn[...], log_dt_gn[...])

    tile_n = _lane_tiler(SSM_STATE, HALF_GROUPS)
    mask_c = _group_mask((MXU_TILE, HALF_STATE), SSM_GROUP_CH, SSM_STATE)

    def per_channel(q, k):
        qk = q[k * HALF_GROUPS:(k + 1) * HALF_GROUPS, :]
        return jnp.broadcast_to(qk[:, None, :], (HALF_GROUPS, SSM_GROUP_CH, SSM_STATE)).reshape(MXU_TILE, SSM_STATE)

    for k, (b_out, c_out) in enumerate(((b_lo, c_lo), (b_hi, c_hi))):
        crow = slice(k * MXU_TILE, (k + 1) * MXU_TILE)
        q_re, q_im = per_channel(qr, k), per_channel(qi, k)
        b_re = b_re_ref[crow, :]
        b_im = b_im_ref[crow, :]
        bbar = (q_re * b_re - q_im * b_im, q_re * b_im + q_im * b_re)
        for j, bb in enumerate(bbar):
            wide = jnp.dot(bb.astype(BF16), tile_n, preferred_element_type=F32)
            blk = jnp.where(mask_c, wide, 0.0).astype(BF16)
            for p in range(LANE_BLOCKS):
                col = _state_col(0, p, j)
                b_out[:, col:col + LANES] = blk[:, p * LANES:(p + 1) * LANES]
        for j, (c_ref, sign) in enumerate(((c_re_ref, 1.0), (c_im_ref, -1.0))):
            wide = jnp.dot(c_ref[crow, :].astype(BF16), tile_n, preferred_element_type=F32)
            blk = jnp.where(mask_c, sign * wide, 0.0).T.astype(BF16)
            for p in range(LANE_BLOCKS):
                row = _state_col(0, p, j)
                c_out[row:row + LANES, :] = blk[p * LANES:(p + 1) * LANES, :]

    mask_p = _group_mask((MXU_TILE, MXU_TILE), POOL_GROUP_CH, POOL_GROUP_CH)
    for k, wp in enumerate((wp_lo, wp_hi)):
        pw = pw_ref[k * MXU_TILE:(k + 1) * MXU_TILE, :]
        wp[...] = jnp.where(mask_p, jnp.concatenate([pw, pw], axis=1), 0.0).astype(BF16)


N_SMALL = 11
N_BIG = 3
N_PREPARED = 12
N_LOADS = 5
N_STORES = 11


def _prep_sample_kernel(*refs):
    small = refs[:N_SMALL]
    w_in_hbm, glu_hbm, w_out_hbm = refs[N_SMALL:N_SMALL + N_BIG]
    base = N_SMALL + N_BIG
    d_ref, glu_b_ref, pscale_ref, ln_g_ref, ln_b_ref, x_hbm, h0_re_t, h0_im_t, prefix_hbm = refs[base:base + 9]
    prepared = refs[base + 9:base + 9 + N_PREPARED]
    (w_s_hbm, w_g_hbm, b_lo_hbm, b_hi_hbm, a_re, a_im, c_lo_hbm, c_hi_hbm, glu16_hbm, wp_lo, wp_hi,
     w_out16_hbm) = prepared
    (y_hbm, hre_t, him_t, pool_hbm, w_in_f, glu_f, w_out_f, prefix_v, w_s16, w_g16, glu16, w_out16, b_lo, b_hi,
     c_lo, c_hi, xs, ys, pin, hb16, sem_in, sem_out) = refs[base + 9 + N_PREPARED:]

    loads = [pltpu.make_async_copy(src, dst, sem_in.at[i]) for i, (src, dst) in enumerate(
        ((w_in_hbm, w_in_f), (w_out_hbm, w_out_f), (glu_hbm, glu_f), (prefix_hbm, prefix_v),
         (x_hbm.at[:, 0, :], xs)))]
    for c in loads:
        c.start()
    stores = []

    def send(src, dst):
        c = pltpu.make_async_copy(src, dst, sem_out.at[len(stores)])
        c.start()
        stores.append(c)

    send(prefix_hbm.at[pl.ds(1, POOL_BUF - 1)], pool_hbm.at[pl.ds(0, POOL_BUF - 1)])

    _prep_small(*small, b_lo, b_hi, a_re, a_im, c_lo, c_hi, wp_lo, wp_hi)
    for src, dst in ((b_lo, b_lo_hbm), (b_hi, b_hi_hbm), (c_lo, c_lo_hbm), (c_hi, c_hi_hbm)):
        send(src, dst)
    loads[0].wait()
    w_s16[...] = w_in_f[:, :SSM_WIDTH].astype(BF16)
    w_g16[...] = w_in_f[:, SSM_WIDTH:].astype(BF16)
    send(w_s16, w_s_hbm)
    send(w_g16, w_g_hbm)
    loads[1].wait()
    w_out16[...] = w_out_f[...].astype(BF16)
    send(w_out16, w_out16_hbm)
    loads[2].wait()
    glu16[...] = glu_f[...].astype(BF16)
    send(glu16, glu16_hbm)
    loads[3].wait()
    loads[4].wait()
    _sample_body(xs, h0_re_t, h0_im_t, prefix_v, w_s16, w_g16, b_lo, b_hi, a_re, a_im, c_lo, c_hi, d_ref,
                 glu16, glu_b_ref, wp_lo, wp_hi, pscale_ref, w_out16, ln_g_ref, ln_b_ref, ys, hre_t, him_t,
                 pin, hb16)
    send(pin, pool_hbm.at[POOL_BUF - 1])
    send(ys, y_hbm.at[:, 0, :])
    assert len(stores) == N_STORES
    for c in stores:
        c.wait()


_PREPARED_ORDER = ('w_s', 'w_g', 'b_lo', 'b_hi', 'a_re', 'a_im', 'c_lo', 'c_hi', 'glu_w', 'wp_lo', 'wp_hi', 'w_out')


def _prepare_and_sample(x, h_re, h_im, pool_prefix, w_in, lam_re, lam_im, log_dt, b_re, b_im, c_re, c_im, d,
                        glu_w, glu_b, pool_w, pool_scale, w_out, ln_g, ln_b):
    bsz = x.shape[0]
    n_cols = SSM_GROUPS * SSM_STATE
    log_dt_cols = jnp.broadcast_to(log_dt.astype(F32)[:, None], (SSM_GROUPS, SSM_STATE))
    row = lambda v: v.astype(F32).reshape(1, -1)
    rows = dict(d=row(d), glu_b=row(glu_b), pscale=row(pool_scale), ln_g=row(ln_g), ln_b=row(ln_b))
    bf = lambda shape: jax.ShapeDtypeStruct(shape, BF16)
    f32 = lambda shape: jax.ShapeDtypeStruct(shape, F32)
    prepared_shape = dict(w_s=bf((D_MODEL, SSM_WIDTH)), w_g=bf((D_MODEL, SSM_WIDTH + 2 * POOL_WIDTH)),
                          b_lo=bf((MXU_TILE, 2 * HALF_STATE)), b_hi=bf((MXU_TILE, 2 * HALF_STATE)),
                          a_re=f32((SUBLANES, n_cols)), a_im=f32((SUBLANES, n_cols)),
                          c_lo=bf((2 * HALF_STATE, MXU_TILE)), c_hi=bf((2 * HALF_STATE, MXU_TILE)),
                          glu_w=bf((SSM_WIDTH, SSM_WIDTH)), wp_lo=bf((MXU_TILE, MXU_TILE)),
                          wp_hi=bf((MXU_TILE, MXU_TILE)), w_out=bf((D_MODEL, D_MODEL)))
    assert tuple(prepared_shape) == _PREPARED_ORDER and len(_PREPARED_ORDER) == N_PREPARED
    by_channel = lambda b: jnp.swapaxes(b.astype(F32), 1, 2).reshape(SSM_WIDTH, SSM_STATE)
    gn = lambda v: v.astype(F32).reshape(SSM_GROUPS, SSM_STATE)
    small = (lam_re.astype(F32).reshape(1, n_cols), lam_im.astype(F32).reshape(1, n_cols),
             log_dt_cols.reshape(1, n_cols), gn(lam_re), gn(lam_im), log_dt_cols,
             by_channel(b_re), by_channel(b_im),
             c_re.astype(F32).reshape(SSM_WIDTH, SSM_STATE), c_im.astype(F32).reshape(SSM_WIDTH, SSM_STATE),
             pool_w.astype(F32).reshape(POOL_WIDTH, POOL_GROUP_CH))
    big = (w_in.astype(F32), glu_w.astype(F32), w_out.astype(F32))
    assert len(small) == N_SMALL and len(big) == N_BIG
    state_t = lambda h: jnp.transpose(h.astype(F32), (1, 2, 0)).reshape(n_cols, bsz)
    prefix = jnp.swapaxes(pool_prefix.astype(F32), 0, 1)
    vmem = pl.BlockSpec(memory_space=pltpu.VMEM)
    hbm = pl.BlockSpec(memory_space=pl.ANY)
    in_vmem = ('a_re', 'a_im', 'wp_lo', 'wp_hi')
    outs = pl.pallas_call(
        _prep_sample_kernel,
        in_specs=[vmem] * N_SMALL + [hbm] * N_BIG + [vmem] * 5 + [hbm, vmem, vmem, hbm],
        out_specs=[vmem if k in in_vmem else hbm for k in _PREPARED_ORDER] + [hbm, vmem, vmem, hbm],
        out_shape=[prepared_shape[k] for k in _PREPARED_ORDER]
        + [f32((bsz, 1, D_MODEL)), f32((n_cols, bsz)), f32((n_cols, bsz)), f32(prefix.shape)],
        scratch_shapes=[pltpu.VMEM(w.shape, F32) for w in (big[0], big[1], big[2], prefix)]
        + [pltpu.VMEM(prepared_shape[k].shape, BF16)
           for k in ('w_s', 'w_g', 'glu_w', 'w_out', 'b_lo', 'b_hi', 'c_lo', 'c_hi')]
        + [pltpu.VMEM((bsz, D_MODEL), F32),
           pltpu.VMEM((bsz, D_MODEL), F32),
           pltpu.VMEM((bsz, POOL_WIDTH), F32),
           pltpu.VMEM((bsz, STATE_COLS), BF16),
           pltpu.SemaphoreType.DMA((N_LOADS,)),
           pltpu.SemaphoreType.DMA((N_STORES,))],
        compiler_params=pltpu.CompilerParams(vmem_limit_bytes=VMEM_LIMIT_BYTES),
        name="prepare_weights_and_sample_layer",
    )(*small, *big, rows['d'], rows['glu_b'], rows['pscale'], rows['ln_g'], rows['ln_b'], x,
      state_t(h_re), state_t(h_im), prefix)
    w = dict(zip(_PREPARED_ORDER, outs[:N_PREPARED]))
    w.update(rows)
    y, n_re_t, n_im_t, buf_t = outs[N_PREPARED:]
    from_t = lambda h: jnp.transpose(h.reshape(SSM_GROUPS, SSM_STATE, bsz), (2, 0, 1))
    return w, y, from_t(n_re_t), from_t(n_im_t), jnp.swapaxes(buf_t, 0, 1)


_WEIGHT_ORDER = ('w_s', 'w_g', 'b_lo', 'b_hi', 'a_re', 'a_im', 'c_lo', 'c_hi', 'd', 'glu_w', 'glu_b',
                 'wp_lo', 'wp_hi', 'pscale', 'w_out', 'ln_g', 'ln_b')


def _resident(shape):
    return pl.BlockSpec(shape, lambda s: (0,) * len(shape), pipeline_mode=pl.Buffered(1))


def _prompt_layer(x, w):
    bsz, seq, _ = x.shape
    assert bsz == SUBLANES and seq % CHUNK_TOKENS == 0 and seq >= X_SLOTS * CHUNK_TOKENS
    rows = CHUNK_TOKENS * SUBLANES
    halo = HALO_TOKENS * SUBLANES
    n_cols = SSM_GROUPS * SSM_STATE
    weights = [w[k] for k in _WEIGHT_ORDER]
    y, h_re, h_im, ptail = pl.pallas_call(
        _prompt_kernel,
        grid=(seq // CHUNK_TOKENS,),
        in_specs=[pl.BlockSpec(memory_space=pl.ANY)] + [_resident(a.shape) for a in weights],
        out_specs=[pl.BlockSpec(memory_space=pl.ANY),
                   pl.BlockSpec((SUBLANES, n_cols), lambda s: (0, 0)),
                   pl.BlockSpec((SUBLANES, n_cols), lambda s: (0, 0)),
                   pl.BlockSpec((halo, POOL_WIDTH), lambda s: (0, 0))],
        out_shape=[jax.ShapeDtypeStruct(x.shape, F32),
                   jax.ShapeDtypeStruct((SUBLANES, n_cols), F32),
                   jax.ShapeDtypeStruct((SUBLANES, n_cols), F32),
                   jax.ShapeDtypeStruct((halo, POOL_WIDTH), F32)],
        scratch_shapes=[pltpu.VMEM((X_SLOTS, CHUNK_TOKENS, SUBLANES, D_MODEL), F32),
                        pltpu.VMEM((2, CHUNK_TOKENS, SUBLANES, D_MODEL), F32),
                        pltpu.VMEM((rows, STATE_COLS), F32),
                        pltpu.VMEM((rows, STATE_COLS), BF16),
                        pltpu.VMEM((halo + rows, POOL_WIDTH), F32),
                        pltpu.SemaphoreType.DMA((X_SLOTS,)),
                        pltpu.SemaphoreType.DMA((2,))],
        compiler_params=pltpu.CompilerParams(dimension_semantics=("arbitrary",),
                                             vmem_limit_bytes=VMEM_LIMIT_BYTES),
        name="prompt_layer",
    )(x, *weights)
    buf = ptail.reshape(HALO_TOKENS, SUBLANES, POOL_WIDTH)[HALO_TOKENS - POOL_BUF:]
    state_shape = (bsz, SSM_GROUPS, SSM_STATE)
    return y, h_re.reshape(state_shape), h_im.reshape(state_shape), jnp.swapaxes(buf, 0, 1)


def kernel(x_prompt, x_sample, state_ssm_re, state_ssm_im, state_pool, w_in, ssm_lambda_re, ssm_lambda_im, ssm_log_dt, ssm_b_re, ssm_b_im, ssm_c_re, ssm_c_im, ssm_d, glu_w, glu_b, pool_w, pool_scale, w_out, ln_g, ln_b):
    out_dtype = x_prompt.dtype
    depth = w_in.shape[0]
    hp = x_prompt.astype(F32)
    hs = x_sample.astype(F32)
    p_re, p_im, p_buf, s_re, s_im, s_buf = [], [], [], [], [], []
    for l in range(depth):
        w, hs, hr, hi, buf = _prepare_and_sample(
            hs, state_ssm_re[l], state_ssm_im[l], state_pool[l], w_in[l], ssm_lambda_re[l], ssm_lambda_im[l],
            ssm_log_dt[l], ssm_b_re[l], ssm_b_im[l], ssm_c_re[l], ssm_c_im[l], ssm_d[l], glu_w[l], glu_b[l],
            pool_w[l], pool_scale[l], w_out[l], ln_g[l], ln_b[l])
        s_re.append(hr); s_im.append(hi); s_buf.append(buf)
        hp, hr, hi, buf = _prompt_layer(hp, w)
        p_re.append(hr); p_im.append(hi); p_buf.append(buf)
    return (hp.astype(out_dtype), hs.astype(out_dtype),
            jnp.stack(p_re), jnp.stack(p_im), jnp.stack(p_buf),
            jnp.stack(s_re), jnp.stack(s_im), jnp.stack(s_buf))
```

```python
import jax
import jax.numpy as jnp
from jax import lax
from jax.experimental import pallas as pl
from jax.experimental.pallas import tpu as pltpu

D_MODEL = 1024
SSM_WIDTH = 512
POOL_WIDTH = 512
SSM_GROUPS = 32
SSM_GROUP_CH = 16
SSM_STATE = 64
POOL_WINDOWS = (2, 4, 8, 16)
POOL_GROUP_CH = 128
POOL_BUF = 15
DN_ALPHA = 2.0 ** 0.25
LN_EPS = 1e-5

SUBLANES = 8
MXU_TILE = 256
HALF_GROUPS = MXU_TILE // SSM_GROUP_CH
HALF_STATE = HALF_GROUPS * SSM_STATE
LANES = 128
STATE_COLS = 4 * HALF_STATE
LANE_BLOCKS = HALF_STATE // LANES
CHUNK_TOKENS = 64
TAIL_BLOCK_TOKENS = (32, 32)
X_SLOTS = 3
HALO_TOKENS = 16
VMEM_LIMIT_BYTES = 48 * 1024 * 1024

F32 = jnp.float32
BF16 = jnp.bfloat16


def _dot(a, b):
    return jnp.dot(a.astype(BF16), b, preferred_element_type=F32)


def _state_col(half, block, im):
    return half * 2 * HALF_STATE + block * 2 * LANES + im * LANES


def _ssm_input(xb, w_s_ref, b_lo_ref, b_hi_ref):
    s_in = jnp.dot(xb, w_s_ref[...], preferred_element_type=F32)
    sb = s_in.astype(BF16)
    bu_lo = jnp.dot(sb[:, :MXU_TILE], b_lo_ref[...], preferred_element_type=F32)
    bu_hi = jnp.dot(sb[:, MXU_TILE:], b_hi_ref[...], preferred_element_type=F32)
    return s_in, bu_lo, bu_hi


def _layer_norm(r, ln_g_ref, ln_b_ref):
    mu = jnp.mean(r, axis=-1, keepdims=True)
    rc = r - mu
    var = jnp.mean(rc * rc, axis=-1, keepdims=True)
    return rc * lax.rsqrt(var + LN_EPS) * ln_g_ref[...] + ln_b_ref[...]


def _mix(x, xb, s_in, hb, pooled, block_rows, w_g_ref, c_lo_ref, c_hi_ref, d_ref, glu_w_ref,
         glu_b_ref, wp_lo_ref, wp_hi_ref, pscale_ref, w_out_ref):
    assert sum(block_rows) == x.shape[0]
    starts = [sum(block_rows[:i]) for i in range(len(block_rows))]
    rs = [slice(a, a + n) for a, n in zip(starts, block_rows)]
    sy = []
    for r in rs:
        y_lo = jnp.dot(hb[r, :2 * HALF_STATE], c_lo_ref[...], preferred_element_type=F32)
        y_hi = jnp.dot(hb[r, 2 * HALF_STATE:], c_hi_ref[...], preferred_element_type=F32)
        sy.append(jnp.concatenate([y_lo, y_hi], axis=-1) + d_ref[...] * s_in[r])
    s_gate = jnp.dot(xb, w_g_ref[:, :SSM_WIDTH], preferred_element_type=F32)
    sy = [jax.nn.gelu(v) for v in sy]
    glu = [_dot(v, glu_w_ref[...]) + glu_b_ref[...] for v in sy]
    p_gate = jnp.dot(xb, w_g_ref[:, SSM_WIDTH + POOL_WIDTH:], preferred_element_type=F32)
    pb = pooled.astype(BF16)
    py = [jnp.concatenate([jnp.dot(pb[r, :MXU_TILE], wp_lo_ref[...], preferred_element_type=F32),
                           jnp.dot(pb[r, MXU_TILE:], wp_hi_ref[...], preferred_element_type=F32)], axis=-1)
          * pscale_ref[...] for r in rs]
    mixed = [jnp.concatenate([sy[i] * jax.nn.sigmoid(glu[i]) * jax.nn.silu(s_gate[r]),
                              py[i] * jax.nn.silu(p_gate[r])], axis=-1) for i, r in enumerate(rs)]
    return [DN_ALPHA * x[r] + _dot(mixed[i], w_out_ref[...]) for i, r in enumerate(rs)]


def _in_copies(x_hbm, xbuf, sem, chunk, slot):
    return [pltpu.make_async_copy(x_hbm.at[b, pl.ds(chunk * CHUNK_TOKENS, CHUNK_TOKENS), :],
                                  xbuf.at[slot, :, b, :], sem.at[slot]) for b in range(SUBLANES)]


def _out_copies(ybuf, y_hbm, sem, chunk, slot):
    return [pltpu.make_async_copy(ybuf.at[slot, :, b, :],
                                  y_hbm.at[b, pl.ds(chunk * CHUNK_TOKENS, CHUNK_TOKENS), :],
                                  sem.at[slot]) for b in range(SUBLANES)]


def _prompt_kernel(x_hbm, w_s_ref, w_g_ref, b_lo_ref, b_hi_ref, a_re_ref, a_im_ref, c_lo_ref, c_hi_ref,
                   d_ref, glu_w_ref, glu_b_ref, wp_lo_ref, wp_hi_ref, pscale_ref, w_out_ref, ln_g_ref,
                   ln_b_ref, y_hbm, hre_ref, him_ref, ptail_ref, xbuf, ybuf, hbuf, hb16, pbuf, sem_in, sem_out):
    s = pl.program_id(0)
    n = pl.num_programs(0)
    xslot = lax.rem(s, X_SLOTS)
    yslot = lax.rem(s, 2)
    ahead = jnp.minimum(s + X_SLOTS - 1, n - 1)
    ahead_slot = lax.rem(s + X_SLOTS - 1, X_SLOTS)
    rows = CHUNK_TOKENS * SUBLANES
    halo = HALO_TOKENS * SUBLANES

    @pl.when(s == 0)
    def _():
        for c0 in range(X_SLOTS - 1):
            for c in _in_copies(x_hbm, xbuf, sem_in, c0, c0):
                c.start()
        hre_ref[...] = jnp.zeros_like(hre_ref)
        him_ref[...] = jnp.zeros_like(him_ref)
        pbuf[0:halo, :] = jnp.zeros((halo, POOL_WIDTH), F32)

    for c in _in_copies(x_hbm, xbuf, sem_in, s, xslot):
        c.wait()

    x = xbuf[xslot].reshape(rows, D_MODEL)
    xb = x.astype(BF16)
    s_in, bu_lo, bu_hi = _ssm_input(xb, w_s_ref, b_lo_ref, b_hi_ref)
    hbuf[:, :2 * HALF_STATE] = bu_lo
    hbuf[:, 2 * HALF_STATE:] = bu_hi

    p_in = jnp.dot(xb, w_g_ref[:, SSM_WIDTH:SSM_WIDTH + POOL_WIDTH], preferred_element_type=F32)
    pbuf[halo:halo + rows, :] = p_in

    for k in range(2):
        for p in range(LANE_BLOCKS):
            c_re, c_im = _state_col(k, p, 0), _state_col(k, p, 1)
            a_col = k * HALF_STATE + p * LANES
            ar = a_re_ref[:, a_col:a_col + LANES]
            ai = a_im_ref[:, a_col:a_col + LANES]
            hr = hre_ref[:, a_col:a_col + LANES]
            hi = him_ref[:, a_col:a_col + LANES]
            for t2 in range(CHUNK_TOKENS // 2):
                pair = []
                for t in (2 * t2, 2 * t2 + 1):
                    r = t * SUBLANES
                    hr, hi = (ar * hr - ai * hi + hbuf[r:r + SUBLANES, c_re:c_re + LANES],
                              ar * hi + ai * hr + hbuf[r:r + SUBLANES, c_im:c_im + LANES])
                    pair.append((hr, hi))
                r2 = 2 * t2 * SUBLANES
                hb16[r2:r2 + 2 * SUBLANES, c_re:c_re + LANES] = jnp.concatenate(
                    [pair[0][0], pair[1][0]], axis=0).astype(BF16)
                hb16[r2:r2 + 2 * SUBLANES, c_im:c_im + LANES] = jnp.concatenate(
                    [pair[0][1], pair[1][1]], axis=0).astype(BF16)
            hre_ref[:, a_col:a_col + LANES] = hr
            him_ref[:, a_col:a_col + LANES] = hi

    row = lax.broadcasted_iota(jnp.int32, (rows, POOL_GROUP_CH), 0)
    pos1 = s * CHUNK_TOKENS + lax.shift_right_logical(row, 3) + 1
    pooled = []
    for g, w in enumerate(POOL_WINDOWS):
        acc = pbuf[:, g * POOL_GROUP_CH:(g + 1) * POOL_GROUP_CH]
        shift = SUBLANES
        while shift < w * SUBLANES:
            acc = acc[shift:, :] + acc[:-shift, :]
            shift *= 2
        acc = acc[acc.shape[0] - rows:, :]
        cnt = jnp.minimum(pos1, w).astype(F32)
        pooled.append(acc / cnt - p_in[:, g * POOL_GROUP_CH:(g + 1) * POOL_GROUP_CH])
    pooled = jnp.concatenate(pooled, axis=-1)
    pbuf[0:halo, :] = pbuf[rows:rows + halo, :]
    ptail_ref[...] = pbuf[0:halo, :]

    res = _mix(x, xb, s_in, hb16, pooled, [t * SUBLANES for t in TAIL_BLOCK_TOKENS], w_g_ref, c_lo_ref, c_hi_ref,
               d_ref, glu_w_ref, glu_b_ref, wp_lo_ref, wp_hi_ref, pscale_ref, w_out_ref)
    t0 = 0
    for blk_tokens, r in zip(TAIL_BLOCK_TOKENS, res):
        y = _layer_norm(r, ln_g_ref, ln_b_ref)
        ybuf[yslot, t0:t0 + blk_tokens] = y.reshape(blk_tokens, SUBLANES, D_MODEL)
        t0 += blk_tokens

    for c in _out_copies(ybuf, y_hbm, sem_out, s, yslot):
        c.start()
    for c in _in_copies(x_hbm, xbuf, sem_in, ahead, ahead_slot):
        c.start()

    @pl.when(s >= 1)
    def _():
        for c in _out_copies(ybuf, y_hbm, sem_out, s - 1, 1 - yslot):
            c.wait()

    @pl.when(s == n - 1)
    def _():
        for c in _out_copies(ybuf, y_hbm, sem_out, s, yslot):
            c.wait()
        for back in range(X_SLOTS - 1):
            for c in _in_copies(x_hbm, xbuf, sem_in, n - 1, lax.rem(s + X_SLOTS - 1 - back, X_SLOTS)):
                c.wait()


def _sample_body(xs, h0_re_t, h0_im_t, prefix_ref, w_s_ref, w_g_ref, b_lo_ref, b_hi_ref, a_re_ref,
                 a_im_ref, c_lo_ref, c_hi_ref, d_ref, glu_w_ref, glu_b_ref, wp_lo_ref, wp_hi_ref, pscale_ref,
                 w_out_ref, ln_g_ref, ln_b_ref, ys, hre_t, him_t, pin_ref, hb16):
    x = xs[...]
    xb = x.astype(BF16)
    s_in, bu_lo, bu_hi = _ssm_input(xb, w_s_ref, b_lo_ref, b_hi_ref)
    bu = (bu_lo, bu_hi)
    for k in range(2):
        for p in range(LANE_BLOCKS):
            c_re, c_im = _state_col(k, p, 0), _state_col(k, p, 1)
            a_col = k * HALF_STATE + p * LANES
            ar = a_re_ref[0:1, a_col:a_col + LANES]
            ai = a_im_ref[0:1, a_col:a_col + LANES]
            hr = h0_re_t[a_col:a_col + LANES, :].T
            hi = h0_im_t[a_col:a_col + LANES, :].T
            b_re, b_im = _state_col(0, p, 0), _state_col(0, p, 1)
            nhr = ar * hr - ai * hi + bu[k][:, b_re:b_re + LANES]
            nhi = ar * hi + ai * hr + bu[k][:, b_im:b_im + LANES]
            hre_t[a_col:a_col + LANES, :] = nhr.T
            him_t[a_col:a_col + LANES, :] = nhi.T
            hb16[:, c_re:c_re + LANES] = nhr.astype(BF16)
            hb16[:, c_im:c_im + LANES] = nhi.astype(BF16)

    p_in = jnp.dot(xb, w_g_ref[:, SSM_WIDTH:SSM_WIDTH + POOL_WIDTH], preferred_element_type=F32)
    pin_ref[...] = p_in
    pooled = []
    for g, w in enumerate(POOL_WINDOWS):
        cols = slice(g * POOL_GROUP_CH, (g + 1) * POOL_GROUP_CH)
        acc = p_in[:, cols]
        for j in range(1, w):
            acc = acc + prefix_ref[POOL_BUF - j, :, cols]
        pooled.append(acc / float(w) - p_in[:, cols])
    pooled = jnp.concatenate(pooled, axis=-1)
    res = _mix(x, xb, s_in, hb16, pooled, [x.shape[0]], w_g_ref, c_lo_ref, c_hi_ref, d_ref, glu_w_ref, glu_b_ref,
               wp_lo_ref, wp_hi_ref, pscale_ref, w_out_ref)[0]
    ys[...] = _layer_norm(res, ln_g_ref, ln_b_ref)


def _group_mask(shape, row_group, col_group):
    r = lax.broadcasted_iota(jnp.int32, shape, 0) // row_group
    c = lax.broadcasted_iota(jnp.int32, shape, 1) // col_group
    return r == c


def _lane_tiler(width, copies):
    shape = (width, width * copies)
    r = lax.broadcasted_iota(jnp.int32, shape, 0)
    c = lax.broadcasted_iota(jnp.int32, shape, 1) % width
    return (r == c).astype(BF16)


def _discretise(lam_re, lam_im, log_dt):
    dt = jnp.exp(log_dt)
    mag = jnp.exp(lam_re * dt)
    ar, ai = mag * jnp.cos(lam_im * dt), mag * jnp.sin(lam_im * dt)
    den = lam_re * lam_re + lam_im * lam_im
    nr = ar - 1.0
    qr = (nr * lam_re + ai * lam_im) / den
    qi = (ai * lam_re - nr * lam_im) / den
    return ar, ai, qr, qi


def _prep_small(lam_re_row, lam_im_row, log_dt_row, lam_re_gn, lam_im_gn, log_dt_gn, b_re_ref, b_im_ref, c_re_ref,
                c_im_ref, pw_ref, b_lo, b_hi, a_re, a_im, c_lo, c_hi, wp_lo, wp_hi):
    ar, ai, _, _ = _discretise(lam_re_row[...], lam_im_row[...], log_dt_row[...])
    n_cols = SSM_GROUPS * SSM_STATE
    a_re[...] = jnp.broadcast_to(ar, (SUBLANES, n_cols))
    a_im[...] = jnp.broadcast_to(ai, (SUBLANES, n_cols))
    _, _, qr, qi = _discretise(lam_re_gn[...], lam_im_gn[...], log_dt_gn[...])

    tile_n = _lane_tiler(SSM_STATE, HALF_GROUPS)
    mask_c = _group_mask((MXU_TILE, HALF_STATE), SSM_GROUP_CH, SSM_STATE)

    def per_channel(q, k):
        qk = q[k * HALF_GROUPS:(k + 1) * HALF_GROUPS, :]
        return jnp.broadcast_to(qk[:, None, :], (HALF_GROUPS, SSM_GROUP_CH, SSM_STATE)).reshape(MXU_TILE, SSM_STATE)

    for k, (b_out, c_out) in enumerate(((b_lo, c_lo), (b_hi, c_hi))):
        crow = slice(k * MXU_TILE, (k + 1) * MXU_TILE)
        q_re, q_im = per_channel(qr, k), per_channel(qi, k)
        b_re = b_re_ref[crow, :]
        b_im = b_im_ref[crow, :]
        bbar = (q_re * b_re - q_im * b_im, q_re * b_im + q_im * b_re)
        for j, bb in enumerate(bbar):
            wide = jnp.dot(bb.astype(BF16), tile_n, preferred_element_type=F32)
            blk = jnp.where(mask_c, wide, 0.0).astype(BF16)
            for p in range(LANE_BLOCKS):
                col = _state_col(0, p, j)
                b_out[:, col:col + LANES] = blk[:, p * LANES:(p + 1) * LANES]
        for j, (c_ref, sign) in enumerate(((c_re_ref, 1.0), (c_im_ref, -1.0))):
            wide = jnp.dot(c_ref[crow, :].astype(BF16), tile_n, preferred_element_type=F32)
            blk = jnp.where(mask_c, sign * wide, 0.0).T.astype(BF16)
            for p in range(LANE_BLOCKS):
                row = _state_col(0, p, j)
                c_out[row:row + LANES, :] = blk[p * LANES:(p + 1) * LANES, :]

    mask_p = _group_mask((MXU_TILE, MXU_TILE), POOL_GROUP_CH, POOL_GROUP_CH)
    for k, wp in enumerate((wp_lo, wp_hi)):
        pw = pw_ref[k * MXU_TILE:(k + 1) * MXU_TILE, :]
        wp[...] = jnp.where(mask_p, jnp.concatenate([pw, pw], axis=1), 0.0).astype(BF16)


N_SMALL = 4
N_BIG = 3
N_ROWS = 5
_ROW_ORDER = (('lam_re', SSM_GROUPS * SSM_STATE), ('lam_im', SSM_GROUPS * SSM_STATE),
              ('log_dt', SSM_GROUPS * SSM_STATE), ('d', SSM_WIDTH), ('glu_b', SSM_WIDTH), ('pscale', POOL_WIDTH),
              ('ln_g', D_MODEL), ('ln_b', D_MODEL))
ROW_SEGMENTS = {name: (sum(n for _, n in _ROW_ORDER[:i]), n) for i, (name, n) in enumerate(_ROW_ORDER)}
ROWS_WIDTH = sum(n for _, n in _ROW_ORDER)
_ROW_NAMES = ('d', 'glu_b', 'pscale', 'ln_g', 'ln_b')
N_PREPARED = 12
N_LOADS = 5
N_STORES = 11


def _prep_sample_kernel(*refs):
    rows_ref, gn_ref, bc_ref, pw_ref = refs[:N_SMALL]
    w_in_hbm, glu_hbm, w_out_hbm = refs[N_SMALL:N_SMALL + N_BIG]
    base = N_SMALL + N_BIG
    x_hbm, h0_re_t, h0_im_t, prefix_hbm = refs[base:base + 4]
    prepared = refs[base + 4:base + 4 + N_PREPARED]
    (w_s_hbm, w_g_hbm, b_lo_hbm, b_hi_hbm, a_re, a_im, c_lo_hbm, c_hi_hbm, glu16_hbm, wp_lo, wp_hi,
     w_out16_hbm) = prepared
    (y_hbm, hre_t, him_t, pool_hbm, d_ref, glu_b_ref, pscale_ref, ln_g_ref, ln_b_ref, w_in_f, glu_f, w_out_f,
     prefix_v, w_s16, w_g16, glu16, w_out16, b_lo, b_hi, c_lo, c_hi, xs, ys, pin, hb16, sem_in,
     sem_out) = refs[base + 4 + N_PREPARED:]

    def row(name):
        start, width = ROW_SEGMENTS[name]
        return rows_ref.at[:, start:start + width]

    for name, out in (('d', d_ref), ('glu_b', glu_b_ref), ('pscale', pscale_ref), ('ln_g', ln_g_ref),
                      ('ln_b', ln_b_ref)):
        out[...] = row(name)[...]
    small = (row('lam_re'), row('lam_im'), row('log_dt'), gn_ref.at[0], gn_ref.at[1], gn_ref.at[2],
             bc_ref.at[0], bc_ref.at[1], bc_ref.at[2], bc_ref.at[3], pw_ref)

    loads = [pltpu.make_async_copy(src, dst, sem_in.at[i]) for i, (src, dst) in enumerate(
        ((w_in_hbm, w_in_f), (w_out_hbm, w_out_f), (glu_hbm, glu_f), (prefix_hbm, prefix_v),
         (x_hbm.at[:, 0, :], xs)))]
    for c in loads:
        c.start()
    stores = []

    def send(src, dst):
        c = pltpu.make_async_copy(src, dst, sem_out.at[len(stores)])
        c.start()
        stores.append(c)


    _prep_small(*small, b_lo, b_hi, a_re, a_im, c_lo, c_hi, wp_lo, wp_hi)
    for src, dst in ((b_lo, b_lo_hbm), (b_hi, b_hi_hbm), (c_lo, c_lo_hbm), (c_hi, c_hi_hbm)):
        send(src, dst)
    loads[0].wait()
    w_s16[...] = w_in_f[:, :SSM_WIDTH].astype(BF16)
    w_g16[...] = w_in_f[:, SSM_WIDTH:].astype(BF16)
    send(w_s16, w_s_hbm)
    send(w_g16, w_g_hbm)
    loads[1].wait()
    w_out16[...] = w_out_f[...].astype(BF16)
    send(w_out16, w_out16_hbm)
    loads[2].wait()
    glu16[...] = glu_f[...].astype(BF16)
    send(glu16, glu16_hbm)
    loads[3].wait()
    send(prefix_v.at[pl.ds(1, POOL_BUF - 1)], pool_hbm.at[pl.ds(0, POOL_BUF - 1)])
    loads[4].wait()
    _sample_body(xs, h0_re_t, h0_im_t, prefix_v, w_s16, w_g16, b_lo, b_hi, a_re, a_im, c_lo, c_hi, d_ref,
                 glu16, glu_b_ref, wp_lo, wp_hi, pscale_ref, w_out16, ln_g_ref, ln_b_ref, ys, hre_t, him_t,
                 pin, hb16)
    send(pin, pool_hbm.at[POOL_BUF - 1])
    send(ys, y_hbm.at[:, 0, :])
    assert len(stores) == N_STORES
    for c in stores:
        c.wait()


_PREPARED_ORDER = ('w_s', 'w_g', 'b_lo', 'b_hi', 'a_re', 'a_im', 'c_lo', 'c_hi', 'glu_w', 'wp_lo', 'wp_hi', 'w_out')


def _prepare_and_sample(x, h_re, h_im, pool_prefix, w_in, lam_re, lam_im, log_dt, b_re, b_im, c_re, c_im, d,
                        glu_w, glu_b, pool_w, pool_scale, w_out, ln_g, ln_b):
    bsz = x.shape[0]
    n_cols = SSM_GROUPS * SSM_STATE
    log_dt_cols = jnp.broadcast_to(log_dt.astype(F32)[:, None], (SSM_GROUPS, SSM_STATE))
    flat = lambda v: v.astype(F32).reshape(-1)
    row_values = dict(lam_re=lam_re, lam_im=lam_im, log_dt=log_dt_cols, d=d, glu_b=glu_b, pscale=pool_scale,
                      ln_g=ln_g, ln_b=ln_b)
    rows_all = jnp.concatenate([flat(row_values[name]) for name, _ in _ROW_ORDER]).reshape(1, ROWS_WIDTH)
    bf = lambda shape: jax.ShapeDtypeStruct(shape, BF16)
    f32 = lambda shape: jax.ShapeDtypeStruct(shape, F32)
    prepared_shape = dict(w_s=bf((D_MODEL, SSM_WIDTH)), w_g=bf((D_MODEL, SSM_WIDTH + 2 * POOL_WIDTH)),
                          b_lo=bf((MXU_TILE, 2 * HALF_STATE)), b_hi=bf((MXU_TILE, 2 * HALF_STATE)),
                          a_re=f32((SUBLANES, n_cols)), a_im=f32((SUBLANES, n_cols)),
                          c_lo=bf((2 * HALF_STATE, MXU_TILE)), c_hi=bf((2 * HALF_STATE, MXU_TILE)),
                          glu_w=bf((SSM_WIDTH, SSM_WIDTH)), wp_lo=bf((MXU_TILE, MXU_TILE)),
                          wp_hi=bf((MXU_TILE, MXU_TILE)), w_out=bf((D_MODEL, D_MODEL)))
    assert tuple(prepared_shape) == _PREPARED_ORDER and len(_PREPARED_ORDER) == N_PREPARED
    by_channel = lambda b: jnp.swapaxes(b.astype(F32), 1, 2).reshape(SSM_WIDTH, SSM_STATE)
    gn = lambda v: v.astype(F32).reshape(SSM_GROUPS, SSM_STATE)
    small = (rows_all, jnp.stack([gn(lam_re), gn(lam_im), log_dt_cols]),
             jnp.stack([by_channel(b_re), by_channel(b_im), c_re.astype(F32).reshape(SSM_WIDTH, SSM_STATE),
                        c_im.astype(F32).reshape(SSM_WIDTH, SSM_STATE)]),
             pool_w.astype(F32).reshape(POOL_WIDTH, POOL_GROUP_CH))
    big = (w_in.astype(F32), glu_w.astype(F32), w_out.astype(F32))
    assert len(small) == N_SMALL and len(big) == N_BIG
    state_t = lambda h: jnp.transpose(h.astype(F32), (1, 2, 0)).reshape(n_cols, bsz)
    prefix = jnp.swapaxes(pool_prefix.astype(F32), 0, 1)
    vmem = pl.BlockSpec(memory_space=pltpu.VMEM)
    hbm = pl.BlockSpec(memory_space=pl.ANY)
    in_vmem = ('a_re', 'a_im', 'wp_lo', 'wp_hi')
    outs = pl.pallas_call(
        _prep_sample_kernel,
        in_specs=[vmem] * N_SMALL + [hbm] * N_BIG + [hbm, vmem, vmem, hbm],
        out_specs=[vmem if k in in_vmem else hbm for k in _PREPARED_ORDER] + [hbm, vmem, vmem, hbm]
        + [vmem] * N_ROWS,
        out_shape=[prepared_shape[k] for k in _PREPARED_ORDER]
        + [f32((bsz, 1, D_MODEL)), f32((n_cols, bsz)), f32((n_cols, bsz)), f32(prefix.shape)]
        + [f32((1, ROW_SEGMENTS[name][1])) for name in _ROW_NAMES],
        scratch_shapes=[pltpu.VMEM(w.shape, F32) for w in (big[0], big[1], big[2], prefix)]
        + [pltpu.VMEM(prepared_shape[k].shape, BF16)
           for k in ('w_s', 'w_g', 'glu_w', 'w_out', 'b_lo', 'b_hi', 'c_lo', 'c_hi')]
        + [pltpu.VMEM((bsz, D_MODEL), F32),
           pltpu.VMEM((bsz, D_MODEL), F32),
           pltpu.VMEM((bsz, POOL_WIDTH), F32),
           pltpu.VMEM((bsz, STATE_COLS), BF16),
           pltpu.SemaphoreType.DMA((N_LOADS,)),
           pltpu.SemaphoreType.DMA((N_STORES,))],
        compiler_params=pltpu.CompilerParams(vmem_limit_bytes=VMEM_LIMIT_BYTES),
        name="prepare_weights_and_sample_layer",
    )(*small, *big, x, state_t(h_re), state_t(h_im), prefix)
    w = dict(zip(_PREPARED_ORDER, outs[:N_PREPARED]))
    w.update(zip(_ROW_NAMES, outs[N_PREPARED + 4:]))
    y, n_re_t, n_im_t, buf_t = outs[N_PREPARED:N_PREPARED + 4]
    from_t = lambda h: jnp.transpose(h.reshape(SSM_GROUPS, SSM_STATE, bsz), (2, 0, 1))
    return w, y, from_t(n_re_t), from_t(n_im_t), jnp.swapaxes(buf_t, 0, 1)


_WEIGHT_ORDER = ('w_s', 'w_g', 'b_lo', 'b_hi', 'a_re', 'a_im', 'c_lo', 'c_hi', 'd', 'glu_w', 'glu_b',
                 'wp_lo', 'wp_hi', 'pscale', 'w_out', 'ln_g', 'ln_b')


def _resident(shape):
    return pl.BlockSpec(shape, lambda s: (0,) * len(shape), pipeline_mode=pl.Buffered(1))


def _prompt_layer(x, w):
    bsz, seq, _ = x.shape
    assert bsz == SUBLANES and seq % CHUNK_TOKENS == 0 and seq >= X_SLOTS * CHUNK_TOKENS
    rows = CHUNK_TOKENS * SUBLANES
    halo = HALO_TOKENS * SUBLANES
    n_cols = SSM_GROUPS * SSM_STATE
    weights = [w[k] for k in _WEIGHT_ORDER]
    y, h_re, h_im, ptail = pl.pallas_call(
        _prompt_kernel,
        grid=(seq // CHUNK_TOKENS,),
        in_specs=[pl.BlockSpec(memory_space=pl.ANY)] + [_resident(a.shape) for a in weights],
        out_specs=[pl.BlockSpec(memory_space=pl.ANY),
                   pl.BlockSpec((SUBLANES, n_cols), lambda s: (0, 0)),
                   pl.BlockSpec((SUBLANES, n_cols), lambda s: (0, 0)),
                   pl.BlockSpec((halo, POOL_WIDTH), lambda s: (0, 0))],
        out_shape=[jax.ShapeDtypeStruct(x.shape, F32),
                   jax.ShapeDtypeStruct((SUBLANES, n_cols), F32),
                   jax.ShapeDtypeStruct((SUBLANES, n_cols), F32),
                   jax.ShapeDtypeStruct((halo, POOL_WIDTH), F32)],
        scratch_shapes=[pltpu.VMEM((X_SLOTS, CHUNK_TOKENS, SUBLANES, D_MODEL), F32),
                        pltpu.VMEM((2, CHUNK_TOKENS, SUBLANES, D_MODEL), F32),
                        pltpu.VMEM((rows, STATE_COLS), F32),
                        pltpu.VMEM((rows, STATE_COLS), BF16),
                        pltpu.VMEM((halo + rows, POOL_WIDTH), F32),
                        pltpu.SemaphoreType.DMA((X_SLOTS,)),
                        pltpu.SemaphoreType.DMA((2,))],
        compiler_params=pltpu.CompilerParams(dimension_semantics=("arbitrary",),
                                             vmem_limit_bytes=VMEM_LIMIT_BYTES),
        name="prompt_layer",
    )(x, *weights)
    buf = ptail.reshape(HALO_TOKENS, SUBLANES, POOL_WIDTH)[HALO_TOKENS - POOL_BUF:]
    state_shape = (bsz, SSM_GROUPS, SSM_STATE)
    return y, h_re.reshape(state_shape), h_im.reshape(state_shape), jnp.swapaxes(buf, 0, 1)


def kernel(x_prompt, x_sample, state_ssm_re, state_ssm_im, state_pool, w_in, ssm_lambda_re, ssm_lambda_im, ssm_log_dt, ssm_b_re, ssm_b_im, ssm_c_re, ssm_c_im, ssm_d, glu_w, glu_b, pool_w, pool_scale, w_out, ln_g, ln_b):
    out_dtype = x_prompt.dtype
    depth = w_in.shape[0]
    hp = x_prompt.astype(F32)
    hs = x_sample.astype(F32)
    p_re, p_im, p_buf, s_re, s_im, s_buf = [], [], [], [], [], []
    for l in range(depth):
        w, hs, hr, hi, buf = _prepare_and_sample(
            hs, state_ssm_re[l], state_ssm_im[l], state_pool[l], w_in[l], ssm_lambda_re[l], ssm_lambda_im[l],
            ssm_log_dt[l], ssm_b_re[l], ssm_b_im[l], ssm_c_re[l], ssm_c_im[l], ssm_d[l], glu_w[l], glu_b[l],
            pool_w[l], pool_scale[l], w_out[l], ln_g[l], ln_b[l])
        s_re.append(hr); s_im.append(hi); s_buf.append(buf)
        hp, hr, hi, buf = _prompt_layer(hp, w)
        p_re.append(hr); p_im.append(hi); p_buf.append(buf)
    return (hp.astype(out_dtype), hs.astype(out_dtype),
            jnp.stack(p_re), jnp.stack(p_im), jnp.stack(p_buf),
            jnp.stack(s_re), jnp.stack(s_im), jnp.stack(s_buf))
```
